```python
import math
import jax, jax.numpy as jnp
from jax import lax
import numpy as np

D_MODEL = 1024
BATCH = 8
SEQ = 2048
DEPTH = 2

CONV_WIDTH = 512
CONV_K = 3
SSM_WIDTH = 512
SSM_GROUP = 16
SSM_GROUPS = SSM_WIDTH // SSM_GROUP
SSM_STATE = 64
DT_MIN = 0.001
DT_MAX = 0.1
ATTN_HEADS = 8
HEAD_DIM = 64
ATTN_WIDTH = ATTN_HEADS * HEAD_DIM
IDX_HEADS = 4
IDX_DIM = 64
TOPK_MAX = 256
Q_BLOCK = 128
ROPE_THETA = 10000.0
N_BRANCH = 3
D_FF_DENSE = 2816
N_EXPERTS = 8
TOP_K_EXPERTS = 2
D_FF_EXPERT = 3584
N_DENSE_LAYERS = (DEPTH + 1) // 2
N_MOE_LAYERS = DEPTH // 2
EPS = 1e-6

IN_SIZES = (CONV_WIDTH, CONV_WIDTH, CONV_WIDTH, SSM_WIDTH, ATTN_WIDTH, HEAD_DIM, HEAD_DIM,
            IDX_HEADS * IDX_DIM, IDX_DIM, IDX_HEADS, N_BRANCH * D_MODEL)
IN_COLS = sum(IN_SIZES)

kernel_name = "hybrid_conv_s5_dsa_moe_block"


def rmsnorm(x, g):
    xf = x.astype(jnp.float32)
    y = xf * lax.rsqrt(jnp.mean(xf * xf, axis=-1, keepdims=True) + EPS)
    return (y * g.astype(jnp.float32)).astype(x.dtype)


def split_columns(p):
    bounds = []
    acc = 0
    for s in IN_SIZES[:-1]:
        acc += s
        bounds.append(acc)
    return jnp.split(p, bounds, axis=-1)


def rope_tables(L, dim):
    pos = jnp.arange(L, dtype=jnp.float32)
    inv = ROPE_THETA ** (-jnp.arange(dim // 2, dtype=jnp.float32) / (dim // 2))
    ang = pos[:, None] * inv[None, :]
    return jnp.cos(ang)[:, None, :], jnp.sin(ang)[:, None, :]


def rope(x, cos, sin):
    xf = x.astype(jnp.float32)
    x1, x2 = jnp.split(xf, 2, axis=-1)
    return jnp.concatenate([x1 * cos - x2 * sin, x2 * cos + x1 * sin], axis=-1).astype(x.dtype)


def short_conv_mixer(u, b_gate, c_gate, conv_w):
    v = c_gate * u
    y = lax.conv_general_dilated(v, conv_w[:, None, :].astype(v.dtype), window_strides=(1,),
                                 padding=[(CONV_K - 1, 0)],
                                 dimension_numbers=("NWC", "WIO", "NWC"),
                                 feature_group_count=CONV_WIDTH)
    return b_gate * y


def s5_mixer(u, lam_re, lam_im, log_dt, b_re, b_im, c_re, c_im, d_skip, w_glu):
    f32 = jnp.float32
    Bsz, L, _ = u.shape
    uf = u.astype(f32)
    ug = uf.reshape(Bsz, L, SSM_GROUPS, SSM_GROUP)
    lr = jnp.minimum(lam_re.astype(f32), -1e-4)
    li = lam_im.astype(f32)
    dt = jnp.exp(log_dt.astype(f32))[:, None]
    mag = jnp.exp(lr * dt)
    ang = li * dt
    ab_re = mag * jnp.cos(ang)
    ab_im = mag * jnp.sin(ang)
    nr = ab_re - 1.0
    ni = ab_im
    den = lr * lr + li * li
    coef_re = (nr * lr + ni * li) / den
    coef_im = (ni * lr - nr * li) / den
    bu_re = jnp.einsum('blgh,gph->blgp', ug, b_re.astype(f32))
    bu_im = jnp.einsum('blgh,gph->blgp', ug, b_im.astype(f32))
    x_re = coef_re * bu_re - coef_im * bu_im
    x_im = coef_re * bu_im + coef_im * bu_re
    a_re = jnp.broadcast_to(ab_re[None, None], (1, L, SSM_GROUPS, SSM_STATE))
    a_im = jnp.broadcast_to(ab_im[None, None], (1, L, SSM_GROUPS, SSM_STATE))

    def combine(e1, e2):
        a1r, a1i, b1r, b1i = e1
        a2r, a2i, b2r, b2i = e2
        return (a2r * a1r - a2i * a1i,
                a2r * a1i + a2i * a1r,
                a2r * b1r - a2i * b1i + b2r,
                a2r * b1i + a2i * b1r + b2i)

    _, _, h_re, h_im = lax.associative_scan(combine, (a_re, a_im, x_re, x_im), axis=1)
    y = (jnp.einsum('blgp,ghp->blgh', h_re, c_re.astype(f32))
         - jnp.einsum('blgp,ghp->blgh', h_im, c_im.astype(f32)))
    y = y.reshape(Bsz, L, SSM_WIDTH) + d_skip.astype(f32) * uf
    y = jax.nn.gelu(y)
    y = y * jax.nn.sigmoid(y @ w_glu.astype(f32))
    return y.astype(u.dtype)


def dsa_mixer(q, k, v, q_idx, k_idx, w_idx, q_gain, k_gain, cos, sin):
    f32 = jnp.float32
    Bsz, L = q.shape[0], q.shape[1]
    topk = min(TOPK_MAX, L // 4)
    q = rope(rmsnorm(q, q_gain), cos, sin).astype(f32)
    k = rope(rmsnorm(k[:, :, None, :], k_gain), cos, sin)[:, :, 0].astype(f32)
    vf = v.astype(f32)
    qi = rope(q_idx, cos, sin).astype(f32)
    ki = rope(k_idx[:, :, None, :], cos, sin)[:, :, 0].astype(f32)
    wi = w_idx.astype(f32) * (IDX_HEADS ** -0.5)
    scale = HEAD_DIM ** -0.5
    key_pos = jnp.arange(L)
    gather = jax.vmap(lambda arr, idx: arr[idx])

    def block(i):
        s0 = i * Q_BLOCK
        qb = lax.dynamic_slice_in_dim(q, s0, Q_BLOCK, axis=1)
        qib = lax.dynamic_slice_in_dim(qi, s0, Q_BLOCK, axis=1)
        wib = lax.dynamic_slice_in_dim(wi, s0, Q_BLOCK, axis=1)
        t = s0 + jnp.arange(Q_BLOCK)
        rel = jax.nn.relu(jnp.einsum('bthd,bsd->bths', qib, ki))
        score = jnp.einsum('bths,bth->bts', rel, wib)
        causal = key_pos[None, :] <= t[:, None]
        score = jnp.where(causal[None], score, -jnp.inf)
        _, sel = lax.top_k(score, topk)
        valid = sel <= t[None, :, None]
        kg = gather(k, sel)
        vg = gather(vf, sel)
        logits = jnp.einsum('bthd,btkd->bthk', qb, kg) * scale
        logits = jnp.where(valid[:, :, None, :], logits, -jnp.inf)
        p = jax.nn.softmax(logits, axis=-1)
        return jnp.einsum('bthk,btkd->bthd', p, vg)

    out = lax.map(block, jnp.arange(L // Q_BLOCK))
    out = jnp.transpose(out, (1, 0, 2, 3, 4)).reshape(Bsz, L, ATTN_WIDTH)
    return out.astype(v.dtype)


def swiglu(h, w1, w3, w2):
    return (jax.nn.silu(h @ w1) * (h @ w3)) @ w2


def moe_ffn(h, router_w, w1, w3, w2):
    logits = (h @ router_w).astype(jnp.float32)
    vals, idx = lax.top_k(logits, TOP_K_EXPERTS)
    probs = jax.nn.softmax(vals, axis=-1)
    gate = jnp.sum(jax.nn.one_hot(idx, N_EXPERTS, dtype=jnp.float32) * probs[..., None], axis=-2)
    gate = gate.astype(h.dtype)
    out = jnp.zeros_like(h)
    for e in range(N_EXPERTS):
        out = out + gate[..., e:e + 1] * swiglu(h, w1[e], w3[e], w2[e])
    return out


def setup_inputs(seed: int = 0) -> dict:
    key = jax.random.key(seed)
    ks = jax.random.split(key, 32)
    nrm = jax.random.normal
    f32 = jnp.float32
    D = D_MODEL
    G, P, H16 = SSM_GROUPS, SSM_STATE, SSM_GROUP
    n_idx = jnp.arange(P, dtype=f32)
    inp = {}
    inp["x"] = nrm(ks[0], (BATCH, SEQ, D), f32)
    inp["ln1_g"] = 1.0 + 0.02 * nrm(ks[1], (DEPTH, D), f32)
    inp["w_in"] = nrm(ks[2], (DEPTH, D, IN_COLS), f32) * D ** -0.5
    inp["conv_w"] = nrm(ks[3], (DEPTH, CONV_K, CONV_WIDTH), f32) * CONV_K ** -0.5
    inp["ssm_lam_re"] = -0.5 + 0.01 * nrm(ks[4], (DEPTH, G, P), f32)
    inp["ssm_lam_im"] = math.pi * n_idx[None, None, :] + 0.01 * nrm(ks[5], (DEPTH, G, P), f32)
    inp["ssm_log_dt"] = jax.random.uniform(ks[6], (DEPTH, G), f32, math.log(DT_MIN), math.log(DT_MAX))
    inp["ssm_b_re"] = nrm(ks[7], (DEPTH, G, P, H16), f32) * (2.0 * H16) ** -0.5
    inp["ssm_b_im"] = nrm(ks[8], (DEPTH, G, P, H16), f32) * (2.0 * H16) ** -0.5
    inp["ssm_c_re"] = nrm(ks[9], (DEPTH, G, H16, P), f32) * (2.0 * P) ** -0.5
    inp["ssm_c_im"] = nrm(ks[10], (DEPTH, G, H16, P), f32) * (2.0 * P) ** -0.5
    inp["ssm_d"] = nrm(ks[11], (DEPTH, SSM_WIDTH), f32)
    inp["ssm_w_glu"] = nrm(ks[12], (DEPTH, SSM_WIDTH, SSM_WIDTH), f32) * SSM_WIDTH ** -0.5
    inp["q_norm_g"] = 1.0 + 0.02 * nrm(ks[13], (DEPTH, HEAD_DIM), f32)
    inp["k_norm_g"] = 1.0 + 0.02 * nrm(ks[14], (DEPTH, HEAD_DIM), f32)
    inp["w_out_a"] = nrm(ks[15], (DEPTH, CONV_WIDTH, D), f32) * CONV_WIDTH ** -0.5
    inp["w_out_b"] = nrm(ks[16], (DEPTH, SSM_WIDTH, D), f32) * SSM_WIDTH ** -0.5
    inp["w_out_c"] = nrm(ks[17], (DEPTH, ATTN_WIDTH, D), f32) * ATTN_WIDTH ** -0.5
    inp["w_o"] = nrm(ks[18], (DEPTH, D, D), f32) * D ** -0.5
    inp["ln2_g"] = 1.0 + 0.02 * nrm(ks[19], (DEPTH, D), f32)
    inp["ffn_w1"] = nrm(ks[20], (N_DENSE_LAYERS, D, D_FF_DENSE), f32) * D ** -0.5
    inp["ffn_w3"] = nrm(ks[21], (N_DENSE_LAYERS, D, D_FF_DENSE), f32) * D ** -0.5
    inp["ffn_w2"] = nrm(ks[22], (N_DENSE_LAYERS, D_FF_DENSE, D), f32) * D_FF_DENSE ** -0.5
    inp["router_w"] = nrm(ks[23], (N_MOE_LAYERS, D, N_EXPERTS), f32) * D ** -0.5
    inp["moe_w1"] = nrm(ks[24], (N_MOE_LAYERS, N_EXPERTS, D, D_FF_EXPERT), f32) * D ** -0.5
    inp["moe_w3"] = nrm(ks[25], (N_MOE_LAYERS, N_EXPERTS, D, D_FF_EXPERT), f32) * D ** -0.5
    inp["moe_w2"] = nrm(ks[26], (N_MOE_LAYERS, N_EXPERTS, D_FF_EXPERT, D), f32) * D_FF_EXPERT ** -0.5
    return inp


def reference(x, ln1_g, w_in, conv_w, ssm_lam_re, ssm_lam_im, ssm_log_dt, ssm_b_re, ssm_b_im,
              ssm_c_re, ssm_c_im, ssm_d, ssm_w_glu, q_norm_g, k_norm_g, w_out_a, w_out_b, w_out_c,
              w_o, ln2_g, ffn_w1, ffn_w3, ffn_w2, router_w, moe_w1, moe_w3, moe_w2):
    Bsz, L, _ = x.shape
    cos, sin = rope_tables(L, HEAD_DIM)
    for layer in range(DEPTH):
        h = rmsnorm(x, ln1_g[layer])
        proj = h @ w_in[layer]
        a_u, a_b, a_c, s_u, q, k, v, qi, ki, wi, gates = split_columns(proj)
        y_a = short_conv_mixer(a_u, a_b, a_c, conv_w[layer]) @ w_out_a[layer]
        y_b = s5_mixer(s_u, ssm_lam_re[layer], ssm_lam_im[layer], ssm_log_dt[layer],
                       ssm_b_re[layer], ssm_b_im[layer], ssm_c_re[layer], ssm_c_im[layer],
                       ssm_d[layer], ssm_w_glu[layer]) @ w_out_b[layer]
        y_c = dsa_mixer(q.reshape(Bsz, L, ATTN_HEADS, HEAD_DIM), k, v,
                        qi.reshape(Bsz, L, IDX_HEADS, IDX_DIM), ki, wi,
                        q_norm_g[layer], k_norm_g[layer], cos, sin) @ w_out_c[layer]
        g = jax.nn.sigmoid(gates.reshape(Bsz, L, N_BRANCH, D_MODEL))
        mixed = g[:, :, 0] * y_a + g[:, :, 1] * y_b + g[:, :, 2] * y_c
        x = x + mixed @ w_o[layer]
        h2 = rmsnorm(x, ln2_g[layer])
        if layer % 2 == 0:
            j = layer // 2
            x = x + swiglu(h2, ffn_w1[j], ffn_w3[j], ffn_w2[j])
        else:
            j = layer // 2
            x = x + moe_ffn(h2, router_w[j], moe_w1[j], moe_w3[j], moe_w2[j])
    return x
```

```python
import functools
import math

import jax
import jax.numpy as jnp
from jax import lax
from jax.experimental import pallas as pl
from jax.experimental.pallas import tpu as pltpu

F32 = jnp.float32
BF16 = jnp.bfloat16

D_MODEL = 1024
CONV_WIDTH = 512
CONV_K = 3
SSM_WIDTH = 512
SSM_GROUP = 16
SSM_GROUPS = 32
SSM_STATE = 64
ATTN_HEADS = 8
HEAD_DIM = 64
ATTN_WIDTH = 512
IDX_HEADS = 4
IDX_DIM = 64
TOPK_MAX = 256
ROPE_THETA = 10000.0
N_EXPERTS = 8
EPS = 1e-6

SUBLANES = 8
LANES = 128
VMEM_LIMIT_BYTES = 56 * 1024 * 1024

COL_CONV = 0
COL_SSM = 1536
COL_Q = 2048
COL_QI = 2560
COL_KV = 2816
COL_KIW = 2944
COL_GATES = 3072
PROJ_COLS = 6144

SSM_HALVES = 2
SSM_HALF_STATES = SSM_GROUPS // SSM_HALVES * SSM_STATE
SSM_HALF_CH = SSM_WIDTH // SSM_HALVES
SSM_COLS = 2 * SSM_GROUPS * SSM_STATE
SCAN_COLS = 512

BISECT_ITERS = 26


def _cparams(sem):
    return pltpu.CompilerParams(dimension_semantics=sem, vmem_limit_bytes=VMEM_LIMIT_BYTES)


def _sigmoid(x):
    return 1.0 / (1.0 + jnp.exp(-x))


def _dot(a, b):
    return jnp.dot(a, b, preferred_element_type=F32)


def _dot_nt(a, b):
    return lax.dot_general(a, b, (((1,), (1,)), ((), ())), preferred_element_type=F32)


INPROJ_ROWS = 256
INPROJ_NCHUNK = 512


def _inproj_kernel(x_ref, g_ref, w_ref, o_ref):
    x = x_ref[...]
    ms = jnp.mean(x * x, axis=-1, keepdims=True)
    h = ((x * lax.rsqrt(ms + EPS)) * g_ref[...]).astype(BF16)
    for n in range(PROJ_COLS // INPROJ_NCHUNK):
        sl = slice(n * INPROJ_NCHUNK, (n + 1) * INPROJ_NCHUNK)
        o_ref[:, sl] = _dot(h, w_ref[:, sl]).astype(o_ref.dtype)


def _inproj(x_tm, ln_g, w_in_p):
    m = x_tm.shape[0]
    return pl.pallas_call(
        _inproj_kernel,
        grid=(m // INPROJ_ROWS,),
        in_specs=[
            pl.BlockSpec((INPROJ_ROWS, D_MODEL), lambda i: (i, 0)),
            pl.BlockSpec((1, D_MODEL), lambda i: (0, 0)),
            pl.BlockSpec((D_MODEL, PROJ_COLS), lambda i: (0, 0)),
        ],
        out_specs=pl.BlockSpec((INPROJ_ROWS, PROJ_COLS), lambda i: (i, 0)),
        out_shape=jax.ShapeDtypeStruct((m, PROJ_COLS), BF16),
        compiler_params=_cparams(("parallel",)),
        name="inproj",
    )(x_tm, ln_g, w_in_p)


MIX_STEPS = 64
MIX_ROWS = MIX_STEPS * SUBLANES
CONV_HALO = (CONV_K - 1) * SUBLANES


def _mixer_kernel(pa_ref, su_ref, g0_ref, g1_ref, cw_ref, bc_ref, are_ref, aim_ref, cc_ref,
                  d_ref, wglu_ref, woa_ref, wob_ref, o_ref, xs_ref, hst_ref, vext_ref):
    i = pl.program_id(0)
    rows = MIX_ROWS

    @pl.when(i == 0)
    def _():
        hst_ref[...] = jnp.zeros_like(hst_ref)
        vext_ref[0:CONV_HALO, :] = jnp.zeros((CONV_HALO, CONV_WIDTH), F32)

    u = pa_ref[:, 0:CONV_WIDTH].astype(F32)
    bg = pa_ref[:, CONV_WIDTH:2 * CONV_WIDTH].astype(F32)
    cg = pa_ref[:, 2 * CONV_WIDTH:3 * CONV_WIDTH].astype(F32)
    v = cg * u
    vext_ref[CONV_HALO:CONV_HALO + rows, :] = v
    y = (cw_ref[0:1, :] * vext_ref[0:rows, :]
         + cw_ref[1:2, :] * vext_ref[SUBLANES:SUBLANES + rows, :]
         + cw_ref[2:3, :] * v)
    vext_ref[0:CONV_HALO, :] = v[rows - CONV_HALO:rows, :]
    y_a = _dot((bg * y).astype(BF16), woa_ref[...])

    su = su_ref[...]
    for j in range(SSM_HALVES):
        xs_ref[:, j * 2 * SSM_HALF_STATES:(j + 1) * 2 * SSM_HALF_STATES] = _dot(
            su[:, j * SSM_HALF_CH:(j + 1) * SSM_HALF_CH], bc_ref[j])

    for j in range(SSM_HALVES):
        for q in range(SSM_HALF_STATES // SCAN_COLS):
            cr = j * 2 * SSM_HALF_STATES + q * SCAN_COLS
            ci = cr + SSM_HALF_STATES
            ca = j * SSM_HALF_STATES + q * SCAN_COLS
            ar = are_ref[:, ca:ca + SCAN_COLS]
            ai = aim_ref[:, ca:ca + SCAN_COLS]

            def step(t, carry, cr=cr, ci=ci, ar=ar, ai=ai):
                hr, hi = carry
                r0 = pl.multiple_of(t * SUBLANES, SUBLANES)
                xr = xs_ref[pl.ds(r0, SUBLANES), cr:cr + SCAN_COLS]
                xi = xs_ref[pl.ds(r0, SUBLANES), ci:ci + SCAN_COLS]
                nr = ar * hr - ai * hi + xr
                ni = ar * hi + ai * hr + xi
                xs_ref[pl.ds(r0, SUBLANES), cr:cr + SCAN_COLS] = nr
                xs_ref[pl.ds(r0, SUBLANES), ci:ci + SCAN_COLS] = ni
                return nr, ni

            hr, hi = lax.fori_loop(
                0, MIX_STEPS, step,
                (hst_ref[:, cr:cr + SCAN_COLS], hst_ref[:, ci:ci + SCAN_COLS]), unroll=4)
            hst_ref[:, cr:cr + SCAN_COLS] = hr
            hst_ref[:, ci:ci + SCAN_COLS] = hi

    ys = []
    for j in range(SSM_HALVES):
        hj = xs_ref[:, j * 2 * SSM_HALF_STATES:(j + 1) * 2 * SSM_HALF_STATES].astype(BF16)
        ys.append(_dot(hj, cc_ref[j]))
    ysum = jnp.concatenate(ys, axis=-1) + d_ref[...] * su.astype(F32)
    yg = jax.nn.gelu(ysum, approximate=True)
    yg = yg * _sigmoid(_dot(yg.astype(BF16), wglu_ref[...]))
    y_b = _dot(yg.astype(BF16), wob_ref[...])

    o_ref[...] = (_sigmoid(g0_ref[...].astype(F32)) * y_a
                  + _sigmoid(g1_ref[...].astype(F32)) * y_b).astype(o_ref.dtype)


def _mixer(proj, conv_w, bc, a_re8, a_im8, cc, d_skip, w_glu, w_out_a, w_out_b):
    m = proj.shape[0]
    const2 = lambda i: (0, 0)
    const3 = lambda i: (0, 0, 0)
    return pl.pallas_call(
        _mixer_kernel,
        grid=(m // MIX_ROWS,),
        in_specs=[
            pl.BlockSpec((MIX_ROWS, 3 * CONV_WIDTH), lambda i: (i, COL_CONV // (3 * CONV_WIDTH))),
            pl.BlockSpec((MIX_ROWS, SSM_WIDTH), lambda i: (i, COL_SSM // SSM_WIDTH)),
            pl.BlockSpec((MIX_ROWS, D_MODEL), lambda i: (i, COL_GATES // D_MODEL)),
            pl.BlockSpec((MIX_ROWS, D_MODEL), lambda i: (i, COL_GATES // D_MODEL + 1)),
            pl.BlockSpec((CONV_K, CONV_WIDTH), const2),
            pl.BlockSpec((SSM_HALVES, SSM_HALF_CH, 2 * SSM_HALF_STATES), const3),
            pl.BlockSpec((SUBLANES, SSM_GROUPS * SSM_STATE), const2),
            pl.BlockSpec((SUBLANES, SSM_GROUPS * SSM_STATE), const2),
            pl.BlockSpec((SSM_HALVES, 2 * SSM_HALF_STATES, SSM_HALF_CH), const3),
            pl.BlockSpec((1, SSM_WIDTH), const2),
            pl.BlockSpec((SSM_WIDTH, SSM_WIDTH), const2),
            pl.BlockSpec((CONV_WIDTH, D_MODEL), const2),
            pl.BlockSpec((SSM_WIDTH, D_MODEL), const2),
        ],
        out_specs=pl.BlockSpec((MIX_ROWS, D_MODEL), lambda i: (i, 0)),
        out_shape=jax.ShapeDtypeStruct((m, D_MODEL), BF16),
        scratch_shapes=[
            pltpu.VMEM((MIX_ROWS, SSM_COLS), F32),
            pltpu.VMEM((SUBLANES, SSM_COLS), F32),
            pltpu.VMEM((MIX_ROWS + CONV_HALO, CONV_WIDTH), F32),
        ],
        compiler_params=_cparams(("arbitrary",)),
        name="conv_s5_mixer",
    )(proj, proj, proj, proj, conv_w, bc, a_re8, a_im8, cc, d_skip, w_glu, w_out_a, w_out_b)


DSA_TQ = 128


def _rope(x, cos2, sin2):
    half = x.shape[-1] // 2
    swapped = jnp.concatenate([x[:, half:], x[:, :half]], axis=-1)
    return x * cos2 + swapped * sin2


def _dsa_kernel(q_ref, qi_ref, kv_ref, kiw_ref, kiwq_ref, cosk_ref, sink_ref, cosq_ref, sinq_ref,
                qg_ref, kg_ref, tri_ref, o_ref, krot_ref, kirot_ref, bias_ref, *, seq, topk):
    i = pl.program_id(1)
    tq = DSA_TQ

    @pl.when(i == 0)
    def _():
        k = kv_ref[:, 0:HEAD_DIM].astype(F32)
        kn = (k * lax.rsqrt(jnp.mean(k * k, axis=-1, keepdims=True) + EPS)) * kg_ref[...]
        krot_ref[...] = _rope(kn, cosk_ref[...], sink_ref[...]).astype(BF16)
        ki = kiw_ref[:, 0:IDX_DIM].astype(F32)
        kirot_ref[...] = _rope(ki, cosk_ref[...], sink_ref[...]).astype(BF16)

    cosq = cosq_ref[...]
    sinq = sinq_ref[...]

    qi = qi_ref[...].astype(F32)
    wq = kiwq_ref[:, IDX_DIM:IDX_DIM + IDX_HEADS].astype(F32) * (IDX_HEADS ** -0.5)
    kirot = kirot_ref[...]
    score = None
    for h in range(IDX_HEADS):
        qh = _rope(qi[:, h * IDX_DIM:(h + 1) * IDX_DIM], cosq, sinq).astype(BF16)
        rel = jnp.maximum(_dot_nt(qh, kirot), 0.0) * wq[:, h:h + 1]
        score = rel if score is None else score + rel
    col = lax.broadcasted_iota(jnp.int32, (tq, seq), 1)
    row = i * tq + lax.broadcasted_iota(jnp.int32, (tq, seq), 0)
    causal = col <= row
    neg_inf = jnp.float32(-jnp.inf)
    score = jnp.where(causal, score, neg_inf)

    t_row = i * tq + lax.broadcasted_iota(jnp.int32, (tq, 1), 0)
    kk = jnp.minimum(t_row + 1, topk).astype(F32)
    hi0 = jnp.max(score, axis=-1, keepdims=True)
    lo0 = jnp.min(jnp.where(causal, score, jnp.float32(jnp.inf)), axis=-1, keepdims=True)

    def count_ge(thr):
        return jnp.sum(jnp.where(score >= thr, 1.0, 0.0), axis=-1, keepdims=True)

    def count_gt(thr):
        return jnp.sum(jnp.where(score > thr, 1.0, 0.0), axis=-1, keepdims=True)

    def bisect(_, carry):
        lo, hi = carry
        mid = 0.5 * lo + 0.5 * hi
        ge = count_ge(mid) >= kk
        return jnp.where(ge, mid, lo), jnp.where(ge, hi, mid)

    lo, _ = lax.fori_loop(0, BISECT_ITERS, bisect, (lo0, hi0))

    def snap_from(lo_v, strict):
        m = (score > lo_v) if strict else (score >= lo_v)
        return jnp.min(jnp.where(m, score, jnp.float32(jnp.inf)), axis=-1, keepdims=True)

    thr0 = snap_from(lo, False)
    above0 = count_gt(thr0)

    def refine_cond(carry):
        thr, above = carry
        return jnp.max(jnp.where(above >= kk, 1.0, 0.0)) > 0.0

    def refine_body(carry):
        thr, above = carry
        need = above >= kk
        thr = jnp.where(need, snap_from(thr, True), thr)
        return thr, count_gt(thr)

    thr, above = lax.while_loop(refine_cond, refine_body, (thr0, above0))

    need = kk - above
    gt = score > thr
    tie = score == thr
    tie_b = jnp.where(tie, 1.0, 0.0).astype(BF16)
    ones_blk = jnp.ones((LANES, LANES), BF16)
    offs = jnp.zeros((tq, LANES), F32)
    for c in range(seq // LANES):
        sl = slice(c * LANES, (c + 1) * LANES)
        before = _dot(tie_b[:, sl], tri_ref[...]) + offs
        sel = gt[:, sl] | (tie[:, sl] & (before < need))
        bias_ref[:, sl] = jnp.where(sel, 0.0, neg_inf)
        offs = offs + _dot(tie_b[:, sl], ones_blk)

    q = q_ref[...].astype(F32)
    krot = krot_ref[...]
    vv = kv_ref[:, HEAD_DIM:2 * HEAD_DIM]
    scale = HEAD_DIM ** -0.5
    for h in range(ATTN_HEADS):
        qh = q[:, h * HEAD_DIM:(h + 1) * HEAD_DIM]
        qn = (qh * lax.rsqrt(jnp.mean(qh * qh, axis=-1, keepdims=True) + EPS)) * qg_ref[...]
        qr = _rope(qn, cosq, sinq).astype(BF16)
        lg = _dot_nt(qr, krot) * scale + bias_ref[...]
        mx = jnp.max(lg, axis=-1, keepdims=True)
        p = jnp.exp(lg - mx)
        den = jnp.sum(p, axis=-1, keepdims=True)
        oh = _dot(p.astype(BF16), vv) / den
        o_ref[:, h * HEAD_DIM:(h + 1) * HEAD_DIM] = oh.astype(o_ref.dtype)


def _dsa(proj_v, cos2, sin2, q_gain, k_gain, tri, batch):
    seq = proj_v.shape[0]
    topk = min(TOPK_MAX, seq // 4)
    nq = PROJ_COLS // ATTN_WIDTH
    nqi = PROJ_COLS // (IDX_HEADS * IDX_DIM)
    n128 = PROJ_COLS // LANES
    kern = functools.partial(_dsa_kernel, seq=seq, topk=topk)
    return pl.pallas_call(
        kern,
        grid=(batch, seq // DSA_TQ),
        in_specs=[
            pl.BlockSpec((DSA_TQ, ATTN_WIDTH), lambda b, i: (i, b * nq + COL_Q // ATTN_WIDTH)),
            pl.BlockSpec((DSA_TQ, IDX_HEADS * IDX_DIM),
                         lambda b, i: (i, b * nqi + COL_QI // (IDX_HEADS * IDX_DIM))),
            pl.BlockSpec((seq, LANES), lambda b, i: (0, b * n128 + COL_KV // LANES)),
            pl.BlockSpec((seq, LANES), lambda b, i: (0, b * n128 + COL_KIW // LANES)),
            pl.BlockSpec((DSA_TQ, LANES), lambda b, i: (i, b * n128 + COL_KIW // LANES)),
            pl.BlockSpec((seq, HEAD_DIM), lambda b, i: (0, 0)),
            pl.BlockSpec((seq, HEAD_DIM), lambda b, i: (0, 0)),
            pl.BlockSpec((DSA_TQ, HEAD_DIM), lambda b, i: (i, 0)),
            pl.BlockSpec((DSA_TQ, HEAD_DIM), lambda b, i: (i, 0)),
            pl.BlockSpec((1, HEAD_DIM), lambda b, i: (0, 0)),
            pl.BlockSpec((1, HEAD_DIM), lambda b, i: (0, 0)),
            pl.BlockSpec((LANES, LANES), lambda b, i: (0, 0)),
        ],
        out_specs=pl.BlockSpec((DSA_TQ, ATTN_WIDTH), lambda b, i: (i, b)),
        out_shape=jax.ShapeDtypeStruct((seq, batch * ATTN_WIDTH), BF16),
        scratch_shapes=[
            pltpu.VMEM((seq, HEAD_DIM), BF16),
            pltpu.VMEM((seq, IDX_DIM), BF16),
            pltpu.VMEM((DSA_TQ, seq), F32),
        ],
        compiler_params=_cparams(("arbitrary", "arbitrary")),
        name="dsa_attention",
    )(proj_v, proj_v, proj_v, proj_v, proj_v, cos2, sin2, cos2, sin2, q_gain, k_gain, tri)


MERGE_ROWS = 512


def _top2_gates(logits):
    lane = lax.broadcasted_iota(jnp.int32, logits.shape, 1)
    neg_inf = jnp.float32(-jnp.inf)
    lg = jnp.where(lane < N_EXPERTS, logits, neg_inf)
    v1 = jnp.max(lg, axis=-1, keepdims=True)
    i1 = jnp.min(jnp.where(lg == v1, lane, LANES), axis=-1, keepdims=True)
    rest = jnp.where(lane == i1, neg_inf, lg)
    v2 = jnp.max(rest, axis=-1, keepdims=True)
    i2 = jnp.min(jnp.where(rest == v2, lane, LANES), axis=-1, keepdims=True)
    e2 = jnp.exp(v2 - v1)
    den = 1.0 + e2
    return jnp.where(lane == i1, 1.0 / den, 0.0) + jnp.where(lane == i2, e2 / den, 0.0)


def _merge_kernel(*refs, with_router):
    if with_router:
        (at_ref, mab_ref, g2_ref, x_ref, woc_ref, wo_ref, ln_ref, rw_ref,
         x1_ref, h2_ref, gate_ref) = refs
    else:
        at_ref, mab_ref, g2_ref, x_ref, woc_ref, wo_ref, ln_ref, x1_ref, h2_ref = refs
    y_c = _dot(at_ref[...], woc_ref[...])
    mixed = mab_ref[...].astype(F32) + _sigmoid(g2_ref[...].astype(F32)) * y_c
    x1 = x_ref[...] + _dot(mixed.astype(BF16), wo_ref[...])
    x1_ref[...] = x1
    ms = jnp.mean(x1 * x1, axis=-1, keepdims=True)
    h2 = ((x1 * lax.rsqrt(ms + EPS)) * ln_ref[...]).astype(BF16)
    h2_ref[...] = h2
    if with_router:
        gate_ref[...] = _top2_gates(_dot(h2, rw_ref[...]))


def _merge(attn, mab, proj, x_tm, w_out_c, w_o, ln2_g, router_w):
    m = x_tm.shape[0]
    with_router = router_w is not None
    row = lambda i: (i, 0)
    const = lambda i: (0, 0)
    in_specs = [
        pl.BlockSpec((MERGE_ROWS, ATTN_WIDTH), row),
        pl.BlockSpec((MERGE_ROWS, D_MODEL), row),
        pl.BlockSpec((MERGE_ROWS, D_MODEL), lambda i: (i, COL_GATES // D_MODEL + 2)),
        pl.BlockSpec((MERGE_ROWS, D_MODEL), row),
        pl.BlockSpec((ATTN_WIDTH, D_MODEL), const),
        pl.BlockSpec((D_MODEL, D_MODEL), const),
        pl.BlockSpec((1, D_MODEL), const),
    ]
    out_specs = [pl.BlockSpec((MERGE_ROWS, D_MODEL), row), pl.BlockSpec((MERGE_ROWS, D_MODEL), row)]
    out_shape = [jax.ShapeDtypeStruct((m, D_MODEL), F32), jax.ShapeDtypeStruct((m, D_MODEL), BF16)]
    args = [attn, mab, proj, x_tm, w_out_c, w_o, ln2_g]
    if with_router:
        in_specs.append(pl.BlockSpec((D_MODEL, LANES), const))
        out_specs.append(pl.BlockSpec((MERGE_ROWS, LANES), row))
        out_shape.append(jax.ShapeDtypeStruct((m, LANES), F32))
        args.append(router_w)
    return pl.pallas_call(
        functools.partial(_merge_kernel, with_router=with_router),
        grid=(m // MERGE_ROWS,),
        in_specs=in_specs,
        out_specs=out_specs,
        out_shape=out_shape,
        compiler_params=_cparams(("parallel",)),
        name="merge_out_router" if with_router else "merge_out",
    )(*args)


FFN_ROWS = 512


def _ffn_kernel(*refs, gated):
    if gated:
        h_ref, x_ref, gate_ref, w1_ref, w3_ref, w2_ref, o_ref, acc_ref = refs
    else:
        h_ref, x_ref, w1_ref, w3_ref, w2_ref, o_ref = refs
    e = pl.program_id(1)
    f = pl.program_id(2)
    nf = pl.num_programs(2)

    @pl.when((e == 0) & (f == 0))
    def _():
        o_ref[...] = x_ref[...]

    h = h_ref[...]
    a = _dot(h, w1_ref[...])
    b = _dot(h, w3_ref[...])
    part = _dot(((a * _sigmoid(a)) * b).astype(BF16), w2_ref[...])
    if not gated:
        o_ref[...] += part
        return

    @pl.when(f == 0)
    def _():
        acc_ref[...] = part

    @pl.when(f > 0)
    def _():
        acc_ref[...] += part

    @pl.when(f == nf - 1)
    def _():
        lane = lax.broadcasted_iota(jnp.int32, gate_ref.shape, 1)
        g = jnp.sum(jnp.where(lane == e, gate_ref[...], 0.0), axis=-1, keepdims=True)
        o_ref[...] += g * acc_ref[...]


def _ffn(h2, x1, gate, w1, w3, w2, ff_tile):
    m = x1.shape[0]
    n_e, _, d_ff = w1.shape
    gated = gate is not None
    row = lambda i, e, f: (i, 0)
    in_specs = [pl.BlockSpec((FFN_ROWS, D_MODEL), row), pl.BlockSpec((FFN_ROWS, D_MODEL), row)]
    args = [h2, x1]
    scratch = []
    if gated:
        in_specs.append(pl.BlockSpec((FFN_ROWS, LANES), row))
        args.append(gate)
        scratch.append(pltpu.VMEM((FFN_ROWS, D_MODEL), F32))
    in_specs += [
        pl.BlockSpec((None, D_MODEL, ff_tile), lambda i, e, f: (e, 0, f)),
        pl.BlockSpec((None, D_MODEL, ff_tile), lambda i, e, f: (e, 0, f)),
        pl.BlockSpec((None, ff_tile, D_MODEL), lambda i, e, f: (e, f, 0)),
    ]
    args += [w1, w3, w2]
    return pl.pallas_call(
        functools.partial(_ffn_kernel, gated=gated),
        grid=(m // FFN_ROWS, n_e, d_ff // ff_tile),
        in_specs=in_specs,
        out_specs=pl.BlockSpec((FFN_ROWS, D_MODEL), row),
        out_shape=jax.ShapeDtypeStruct((m, D_MODEL), F32),
        scratch_shapes=scratch,
        compiler_params=_cparams(("parallel", "arbitrary", "arbitrary")),
        name="moe_ffn" if gated else "dense_ffn",
    )(*args)


def _pad_in_proj(w_in):
    d = w_in.shape[0]
    c_q_end = 2560
    c_k, c_v_end = 2560, 2688
    c_qi, c_qi_end = 2688, 2944
    c_ki, c_wi_end = 2944, 3012
    pad = jnp.zeros((d, COL_GATES - (COL_KIW + IDX_DIM + IDX_HEADS)), w_in.dtype)
    return jnp.concatenate(
        [w_in[:, :c_q_end], w_in[:, c_qi:c_qi_end], w_in[:, c_k:c_v_end],
         w_in[:, c_ki:c_wi_end], pad, w_in[:, c_wi_end:]], axis=1).astype(BF16)


def _s5_params(lam_re, lam_im, log_dt, b_re, b_im, c_re, c_im):
    lr = jnp.minimum(lam_re.astype(F32), -1e-4)
    li = lam_im.astype(F32)
    dt = jnp.exp(log_dt.astype(F32))[:, None]
    mag = jnp.exp(lr * dt)
    ang = li * dt
    ab_re = mag * jnp.cos(ang)
    ab_im = mag * jnp.sin(ang)
    nr = ab_re - 1.0
    ni = ab_im
    den = lr * lr + li * li
    coef_re = (nr * lr + ni * li) / den
    coef_im = (ni * lr - nr * li) / den
    bf_re = coef_re[:, :, None] * b_re - coef_im[:, :, None] * b_im
    bf_im = coef_re[:, :, None] * b_im + coef_im[:, :, None] * b_re
    gh = SSM_GROUPS // SSM_HALVES
    eye = jnp.eye(gh, dtype=F32)

    def blockdiag_b(w):
        w = w.reshape(SSM_HALVES, gh, SSM_STATE, SSM_GROUP)
        return jnp.einsum('jgph,gk->jghkp', w, eye).reshape(SSM_HALVES, SSM_HALF_CH, SSM_HALF_STATES)

    def blockdiag_c(w):
        w = w.reshape(SSM_HALVES, gh, SSM_GROUP, SSM_STATE)
        return jnp.einsum('jghp,gk->jgpkh', w, eye).reshape(SSM_HALVES, SSM_HALF_STATES, SSM_HALF_CH)

    bc = jnp.concatenate([blockdiag_b(bf_re), blockdiag_b(bf_im)], axis=2).astype(BF16)
    cc = jnp.concatenate([blockdiag_c(c_re.astype(F32)), -blockdiag_c(c_im.astype(F32))],
                         axis=1).astype(BF16)
    a_re8 = jnp.broadcast_to(ab_re.reshape(1, -1), (SUBLANES, SSM_GROUPS * SSM_STATE))
    a_im8 = jnp.broadcast_to(ab_im.reshape(1, -1), (SUBLANES, SSM_GROUPS * SSM_STATE))
    return bc, a_re8, a_im8, cc


def _rope_tables(seq):
    pos = jnp.arange(seq, dtype=F32)
    inv = ROPE_THETA ** (-jnp.arange(HEAD_DIM // 2, dtype=F32) / (HEAD_DIM // 2))
    ang = pos[:, None] * inv[None, :]
    cos, sin = jnp.cos(ang), jnp.sin(ang)
    return jnp.concatenate([cos, cos], axis=-1), jnp.concatenate([-sin, sin], axis=-1)


def kernel(x, ln1_g, w_in, conv_w, ssm_lam_re, ssm_lam_im, ssm_log_dt, ssm_b_re, ssm_b_im,
           ssm_c_re, ssm_c_im, ssm_d, ssm_w_glu, q_norm_g, k_norm_g, w_out_a, w_out_b, w_out_c,
           w_o, ln2_g, ffn_w1, ffn_w3, ffn_w2, router_w, moe_w1, moe_w3, moe_w2):
    batch, seq, d = x.shape
    assert batch == SUBLANES and d == D_MODEL
    depth = w_in.shape[0]
    m = batch * seq
    cos2, sin2 = _rope_tables(seq)
    tri = (lax.broadcasted_iota(jnp.int32, (LANES, LANES), 0)
           < lax.broadcasted_iota(jnp.int32, (LANES, LANES), 1)).astype(BF16)

    xt = jnp.transpose(x, (1, 0, 2)).reshape(m, d)
    for layer in range(depth):
        proj = _inproj(xt, ln1_g[layer][None, :], _pad_in_proj(w_in[layer]))
        bc, a_re8, a_im8, cc = _s5_params(
            ssm_lam_re[layer], ssm_lam_im[layer], ssm_log_dt[layer], ssm_b_re[layer],
            ssm_b_im[layer], ssm_c_re[layer], ssm_c_im[layer])
        mab = _mixer(proj, conv_w[layer], bc, a_re8, a_im8, cc, ssm_d[layer][None, :],
                     ssm_w_glu[layer].astype(BF16), w_out_a[layer].astype(BF16),
                     w_out_b[layer].astype(BF16))
        attn = _dsa(proj.reshape(seq, batch * PROJ_COLS), cos2, sin2, q_norm_g[layer][None, :],
                    k_norm_g[layer][None, :], tri, batch)
        attn = attn.reshape(m, ATTN_WIDTH)
        j = layer // 2
        if layer % 2 == 0:
            x1, h2 = _merge(attn, mab, proj, xt, w_out_c[layer].astype(BF16),
                            w_o[layer].astype(BF16), ln2_g[layer][None, :], None)
            xt = _ffn(h2, x1, None, ffn_w1[j][None].astype(BF16), ffn_w3[j][None].astype(BF16),
                      ffn_w2[j][None].astype(BF16), ff_tile=1408)
        else:
            rw = jnp.pad(router_w[j], ((0, 0), (0, LANES - N_EXPERTS))).astype(BF16)
            x1, h2, gate = _merge(attn, mab, proj, xt, w_out_c[layer].astype(BF16),
                                  w_o[layer].astype(BF16), ln2_g[layer][None, :], rw)
            xt = _ffn(h2, x1, gate, moe_w1[j].astype(BF16), moe_w3[j].astype(BF16),
                      moe_w2[j].astype(BF16), ff_tile=1792)
    return jnp.transpose(xt.reshape(seq, batch, d), (1, 0, 2))
```

```python
import functools

import jax
import jax.numpy as jnp
from jax import lax
from jax.experimental import pallas as pl
from jax.experimental.pallas import tpu as pltpu

F32 = jnp.float32
BF16 = jnp.bfloat16

D_MODEL = 1024
CONV_WIDTH = 512
CONV_K = 3
SSM_WIDTH = 512
SSM_GROUP = 16
SSM_GROUPS = 32
SSM_STATE = 64
ATTN_HEADS = 8
HEAD_DIM = 64
ATTN_WIDTH = 512
IDX_HEADS = 4
IDX_DIM = 64
TOPK_MAX = 256
ROPE_THETA = 10000.0
N_EXPERTS = 8
EPS = 1e-6

SUBLANES = 8
LANES = 128
VMEM_LIMIT_BYTES = 56 * 1024 * 1024

COL_CONV = 0
COL_SSM = 1536
COL_Q = 2048
COL_QI = 2560
COL_KV = 2816
COL_KIW = 2944
COL_GATES = 3072
PROJ_COLS = 6144

SSM_HALVES = 2
SSM_HALF_STATES = SSM_GROUPS // SSM_HALVES * SSM_STATE
SSM_HALF_CH = SSM_WIDTH // SSM_HALVES
SSM_COLS = 2 * SSM_GROUPS * SSM_STATE
SCAN_COLS = 512

BISECT_ITERS = 24


def _cparams(sem):
    return pltpu.CompilerParams(dimension_semantics=sem, vmem_limit_bytes=VMEM_LIMIT_BYTES)


def _sigmoid(x):
    return 1.0 / (1.0 + jnp.exp(-x))


def _dot(a, b):
    return jnp.dot(a, b, preferred_element_type=F32)


def _dot_nt(a, b):
    return lax.dot_general(a, b, (((1,), (1,)), ((), ())), preferred_element_type=F32)


INPROJ_ROWS = 256
INPROJ_NCHUNK = 512


def _rope_block(x, cos, sin):
    lane = lax.broadcasted_iota(jnp.int32, x.shape, 1)
    first_half = (lane % HEAD_DIM) < (HEAD_DIM // 2)
    swapped = jnp.where(first_half, pltpu.roll(x, LANES - HEAD_DIM // 2, 1),
                        pltpu.roll(x, HEAD_DIM // 2, 1))
    return x * cos + swapped * sin


def _head_inv_rms(x, gsum):
    sq = x * x
    hi = sq.astype(BF16)
    lo = (sq - hi.astype(F32)).astype(BF16)
    ss = _dot(hi, gsum) + _dot(lo, gsum)
    return lax.rsqrt(ss * (1.0 / HEAD_DIM) + EPS)


def _inproj_kernel(x_ref, g_ref, w_ref, taba_ref, tabb_ref, qg_ref, kg_ref, gsum_ref, o_ref):
    x = x_ref[...]
    ms = jnp.mean(x * x, axis=-1, keepdims=True)
    h = ((x * lax.rsqrt(ms + EPS)) * g_ref[...]).astype(BF16)
    gsum = gsum_ref[...]
    cos_a, sin_a = taba_ref[:, 0:LANES], taba_ref[:, LANES:2 * LANES]
    cos_b, sin_b = tabb_ref[:, 0:LANES], tabb_ref[:, LANES:2 * LANES]
    for n in range(PROJ_COLS // INPROJ_NCHUNK):
        c0 = n * INPROJ_NCHUNK
        res = _dot(h, w_ref[:, c0:c0 + INPROJ_NCHUNK])
        if c0 == COL_Q:
            for blk in range(INPROJ_NCHUNK // LANES):
                xb = res[:, blk * LANES:(blk + 1) * LANES]
                xb = (xb * _head_inv_rms(xb, gsum)) * qg_ref[...]
                xb = _rope_block(xb, cos_a, sin_a) * (HEAD_DIM ** -0.5)
                o_ref[:, c0 + blk * LANES:c0 + (blk + 1) * LANES] = xb.astype(o_ref.dtype)
        elif c0 == COL_QI:
            for blk in range(INPROJ_NCHUNK // LANES):
                xb = res[:, blk * LANES:(blk + 1) * LANES]
                col = c0 + blk * LANES
                if col < COL_KV:
                    xb = _rope_block(xb, cos_a, sin_a)
                elif col == COL_KV:
                    lane = lax.broadcasted_iota(jnp.int32, xb.shape, 1)
                    xn = (xb * _head_inv_rms(xb, gsum)) * kg_ref[...]
                    xb = _rope_block(jnp.where(lane < HEAD_DIM, xn, xb), cos_b, sin_b)
                else:
                    xb = _rope_block(xb, cos_b, sin_b)
                o_ref[:, col:col + LANES] = xb.astype(o_ref.dtype)
        else:
            o_ref[:, c0:c0 + INPROJ_NCHUNK] = res.astype(o_ref.dtype)


def _inproj(x2d, ln_g, w_in_p, tab_a, tab_b, q_gain2, k_gain2, gsum, seq):
    m = x2d.shape[0]
    tiles_per_seq = seq // INPROJ_ROWS
    const = lambda i: (0, 0)
    return pl.pallas_call(
        _inproj_kernel,
        grid=(m // INPROJ_ROWS,),
        in_specs=[
            pl.BlockSpec((INPROJ_ROWS, D_MODEL), lambda i: (i, 0)),
            pl.BlockSpec((1, D_MODEL), const),
            pl.BlockSpec((D_MODEL, PROJ_COLS), const),
            pl.BlockSpec((INPROJ_ROWS, 2 * LANES), lambda i: (i % tiles_per_seq, 0)),
            pl.BlockSpec((INPROJ_ROWS, 2 * LANES), lambda i: (i % tiles_per_seq, 0)),
            pl.BlockSpec((1, LANES), const),
            pl.BlockSpec((1, LANES), const),
            pl.BlockSpec((LANES, LANES), const),
        ],
        out_specs=pl.BlockSpec((INPROJ_ROWS, PROJ_COLS), lambda i: (i, 0)),
        out_shape=jax.ShapeDtypeStruct((m, PROJ_COLS), BF16),
        compiler_params=_cparams(("parallel",)),
        name="inproj",
    )(x2d, ln_g, w_in_p, tab_a, tab_b, q_gain2, k_gain2, gsum)


MIX_STEPS = 64
MIX_ROWS = MIX_STEPS * SUBLANES


def _mixer_kernel(pa_ref, su_ref, g0_ref, g1_ref, cw_ref, perm_ref, permt_ref, bc_ref, are_ref,
                  aim_ref, cc_ref, d_ref, wglu_ref, woa_ref, wob_ref, o_ref,
                  xs_ref, hst_ref, vext_ref):
    i = pl.program_id(0)
    rows = MIX_ROWS
    tc = MIX_STEPS

    @pl.when(i == 0)
    def _():
        hst_ref[...] = jnp.zeros_like(hst_ref)
        vext_ref[:, 0:SUBLANES, :] = jnp.zeros((SUBLANES, SUBLANES, CONV_WIDTH), F32)

    u = pa_ref[:, :, 0:CONV_WIDTH].astype(F32)
    bg = pa_ref[:, :, CONV_WIDTH:2 * CONV_WIDTH].astype(F32)
    cg = pa_ref[:, :, 2 * CONV_WIDTH:3 * CONV_WIDTH].astype(F32)
    v = cg * u
    vext_ref[:, SUBLANES:SUBLANES + tc, :] = v
    y = (cw_ref[0:1, :] * vext_ref[:, SUBLANES - 2:SUBLANES - 2 + tc, :]
         + cw_ref[1:2, :] * vext_ref[:, SUBLANES - 1:SUBLANES - 1 + tc, :]
         + cw_ref[2:3, :] * v)
    vext_ref[:, 0:SUBLANES, :] = v[:, tc - SUBLANES:tc, :]
    y_a = _dot((bg * y).reshape(rows, CONV_WIDTH).astype(BF16), woa_ref[...])

    u_tm = _dot(perm_ref[...], su_ref[...].reshape(rows, SSM_WIDTH))
    u_tm_b = u_tm.astype(BF16)
    for j in range(SSM_HALVES):
        xs_ref[:, j * 2 * SSM_HALF_STATES:(j + 1) * 2 * SSM_HALF_STATES] = _dot(
            u_tm_b[:, j * SSM_HALF_CH:(j + 1) * SSM_HALF_CH], bc_ref[j])

    for j in range(SSM_HALVES):
        for q in range(SSM_HALF_STATES // SCAN_COLS):
            cr = j * 2 * SSM_HALF_STATES + q * SCAN_COLS
            ci = cr + SSM_HALF_STATES
            ca = j * SSM_HALF_STATES + q * SCAN_COLS
            ar = are_ref[:, ca:ca + SCAN_COLS]
            ai = aim_ref[:, ca:ca + SCAN_COLS]

            def step(t, carry, cr=cr, ci=ci, ar=ar, ai=ai):
                hr, hi = carry
                r0 = pl.multiple_of(t * SUBLANES, SUBLANES)
                xr = xs_ref[pl.ds(r0, SUBLANES), cr:cr + SCAN_COLS]
                xi = xs_ref[pl.ds(r0, SUBLANES), ci:ci + SCAN_COLS]
                nr = ar * hr - ai * hi + xr
                ni = ar * hi + ai * hr + xi
                xs_ref[pl.ds(r0, SUBLANES), cr:cr + SCAN_COLS] = nr
                xs_ref[pl.ds(r0, SUBLANES), ci:ci + SCAN_COLS] = ni
                return nr, ni

            hr, hi = lax.fori_loop(
                0, MIX_STEPS, step,
                (hst_ref[:, cr:cr + SCAN_COLS], hst_ref[:, ci:ci + SCAN_COLS]), unroll=4)
            hst_ref[:, cr:cr + SCAN_COLS] = hr
            hst_ref[:, ci:ci + SCAN_COLS] = hi

    ys = []
    for j in range(SSM_HALVES):
        hj = xs_ref[:, j * 2 * SSM_HALF_STATES:(j + 1) * 2 * SSM_HALF_STATES].astype(BF16)
        ys.append(_dot(hj, cc_ref[j]))
    ysum = jnp.concatenate(ys, axis=-1) + d_ref[...] * u_tm
    yg = jax.nn.gelu(ysum, approximate=True)
    yg = (yg * _sigmoid(_dot(yg.astype(BF16), wglu_ref[...]))).astype(BF16)
    yg_bm = _dot(permt_ref[...], yg).astype(BF16)
    y_b = _dot(yg_bm, wob_ref[...])

    g0 = g0_ref[...].reshape(rows, D_MODEL).astype(F32)
    g1 = g1_ref[...].reshape(rows, D_MODEL).astype(F32)
    out = _sigmoid(g0) * y_a + _sigmoid(g1) * y_b
    o_ref[...] = out.reshape(SUBLANES, tc, D_MODEL).astype(o_ref.dtype)


def _mixer(proj3, conv_w, perm, perm_t, bc, a_re8, a_im8, cc, d_skip, w_glu, w_out_a, w_out_b):
    batch, seq, _ = proj3.shape
    const2 = lambda i: (0, 0)
    const3 = lambda i: (0, 0, 0)
    return pl.pallas_call(
        _mixer_kernel,
        grid=(seq // MIX_STEPS,),
        in_specs=[
            pl.BlockSpec((batch, MIX_STEPS, 3 * CONV_WIDTH),
                         lambda i: (0, i, COL_CONV // (3 * CONV_WIDTH))),
            pl.BlockSpec((batch, MIX_STEPS, SSM_WIDTH), lambda i: (0, i, COL_SSM // SSM_WIDTH)),
            pl.BlockSpec((batch, MIX_STEPS, D_MODEL), lambda i: (0, i, COL_GATES // D_MODEL)),
            pl.BlockSpec((batch, MIX_STEPS, D_MODEL), lambda i: (0, i, COL_GATES // D_MODEL + 1)),
            pl.BlockSpec((CONV_K, CONV_WIDTH), const2),
            pl.BlockSpec((MIX_ROWS, MIX_ROWS), const2),
            pl.BlockSpec((MIX_ROWS, MIX_ROWS), const2),
            pl.BlockSpec((SSM_HALVES, SSM_HALF_CH, 2 * SSM_HALF_STATES), const3),
            pl.BlockSpec((SUBLANES, SSM_GROUPS * SSM_STATE), const2),
            pl.BlockSpec((SUBLANES, SSM_GROUPS * SSM_STATE), const2),
            pl.BlockSpec((SSM_HALVES, 2 * SSM_HALF_STATES, SSM_HALF_CH), const3),
            pl.BlockSpec((1, SSM_WIDTH), const2),
            pl.BlockSpec((SSM_WIDTH, SSM_WIDTH), const2),
            pl.BlockSpec((CONV_WIDTH, D_MODEL), const2),
            pl.BlockSpec((SSM_WIDTH, D_MODEL), const2),
        ],
        out_specs=pl.BlockSpec((batch, MIX_STEPS, D_MODEL), lambda i: (0, i, 0)),
        out_shape=jax.ShapeDtypeStruct((batch, seq, D_MODEL), BF16),
        scratch_shapes=[
            pltpu.VMEM((MIX_ROWS, SSM_COLS), F32),
            pltpu.VMEM((SUBLANES, SSM_COLS), F32),
            pltpu.VMEM((SUBLANES, MIX_STEPS + SUBLANES, CONV_WIDTH), F32),
        ],
        compiler_params=_cparams(("arbitrary",)),
        name="conv_s5_mixer",
    )(proj3, proj3, proj3, proj3, conv_w, perm, perm_t, bc, a_re8, a_im8, cc, d_skip, w_glu,
      w_out_a, w_out_b)


DSA_TQ = 128
DSA_CLASSES = 4
DSA_CHAINS = 2


def _dsa_kernel(q_ref, qi_ref, kv_ref, kiw_ref, kiwq_ref, tri_ref, o_ref,
                vext_ref, score_ref, bias_ref, *, width, topk, row0):
    i = pl.program_id(1)
    tq = DSA_TQ
    rows_c = tq // DSA_CHAINS
    neg_inf = jnp.float32(-jnp.inf)
    pos_inf = jnp.float32(jnp.inf)

    @pl.when(i == 0)
    def _():
        lane = lax.broadcasted_iota(jnp.int32, (width, LANES), 1)
        shifted = pltpu.roll(kv_ref[...].astype(F32), HEAD_DIM, 1)
        vext_ref[...] = jnp.where(lane < HEAD_DIM, shifted, 1.0).astype(BF16)

    t0 = row0 + i * tq

    wq = kiwq_ref[:, IDX_DIM:IDX_DIM + IDX_HEADS].astype(F32) * (IDX_HEADS ** -0.5)
    kirot = kiw_ref[:, 0:IDX_DIM]
    score = None
    for h in range(IDX_HEADS):
        rel = jnp.maximum(_dot_nt(qi_ref[:, h * IDX_DIM:(h + 1) * IDX_DIM], kirot), 0.0)
        rel = rel * wq[:, h:h + 1]
        score = rel if score is None else score + rel
    col = lax.broadcasted_iota(jnp.int32, (tq, width), 1)
    row = t0 + lax.broadcasted_iota(jnp.int32, (tq, width), 0)
    score_ref[...] = jnp.where(col <= row, score, neg_inf)

    def chain_rows(c):
        return slice(c * rows_c, (c + 1) * rows_c)

    kks, los, his = [], [], []
    for c in range(DSA_CHAINS):
        sc = score_ref[chain_rows(c), :]
        t_row = t0 + c * rows_c + lax.broadcasted_iota(jnp.int32, (rows_c, 1), 0)
        kks.append(jnp.minimum(t_row + 1, topk).astype(F32))
        his.append(jnp.max(sc, axis=-1, keepdims=True))
        los.append(jnp.min(jnp.where(sc == neg_inf, pos_inf, sc), axis=-1, keepdims=True))

    def bisect(_, carry):
        los_c, his_c = carry
        new_lo, new_hi = [], []
        for c in range(DSA_CHAINS):
            sc = score_ref[chain_rows(c), :]
            mid = 0.5 * los_c[c] + 0.5 * his_c[c]
            cnt = jnp.sum(jnp.where(sc >= mid, 1.0, 0.0), axis=-1, keepdims=True)
            ge = cnt >= kks[c]
            new_lo.append(jnp.where(ge, mid, los_c[c]))
            new_hi.append(jnp.where(ge, his_c[c], mid))
        return tuple(new_lo), tuple(new_hi)

    los, _ = lax.fori_loop(0, BISECT_ITERS, bisect, (tuple(los), tuple(his)))
    lo = jnp.concatenate(los, axis=0)
    kk = jnp.concatenate(kks, axis=0)

    def count_gt(thr):
        return jnp.sum(jnp.where(score_ref[...] > thr, 1.0, 0.0), axis=-1, keepdims=True)

    def snap_from(lo_v, strict):
        sc = score_ref[...]
        m = (sc > lo_v) if strict else (sc >= lo_v)
        return jnp.min(jnp.where(m, sc, pos_inf), axis=-1, keepdims=True)

    thr0 = snap_from(lo, False)
    above0 = count_gt(thr0)

    def refine_cond(carry):
        _, above = carry
        return jnp.max(jnp.where(above >= kk, 1.0, 0.0)) > 0.0

    def refine_body(carry):
        thr, above = carry
        thr = jnp.where(above >= kk, snap_from(thr, True), thr)
        return thr, count_gt(thr)

    thr, above = lax.while_loop(refine_cond, refine_body, (thr0, above0))

    need = kk - above
    ones_blk = jnp.ones((LANES, LANES), BF16)
    offs = jnp.zeros((tq, LANES), F32)
    for c in range(width // LANES):
        sl = slice(c * LANES, (c + 1) * LANES)
        sc = score_ref[:, sl]
        tie = sc == thr
        tie_b = jnp.where(tie, 1.0, 0.0).astype(BF16)
        before = _dot(tie_b, tri_ref[...]) + offs
        sel = (sc > thr) | (tie & (before < need))
        bias_ref[:, sl] = jnp.where(sel, 0.0, neg_inf)
        offs = offs + _dot(tie_b, ones_blk)

    krot = kv_ref[:, 0:HEAD_DIM]
    vext = vext_ref[...]
    for h in range(ATTN_HEADS):
        lg = _dot_nt(q_ref[:, h * HEAD_DIM:(h + 1) * HEAD_DIM], krot) + bias_ref[...]
        mx = jnp.max(lg, axis=-1, keepdims=True)
        p = jnp.exp(lg - mx).astype(BF16)
        ov = _dot(p, vext)
        oh = ov[:, 0:HEAD_DIM] / ov[:, HEAD_DIM:HEAD_DIM + 1]
        o_ref[:, h * HEAD_DIM:(h + 1) * HEAD_DIM] = oh.astype(o_ref.dtype)


def _dsa(proj3, tri):
    batch, seq, _ = proj3.shape
    topk = min(TOPK_MAX, seq // 4)
    class_len = seq // DSA_CLASSES
    tiles = class_len // DSA_TQ
    outs = []
    for c in range(DSA_CLASSES):
        width = (c + 1) * class_len
        base = c * tiles
        kern = functools.partial(_dsa_kernel, width=width, topk=topk, row0=c * class_len)
        outs.append(pl.pallas_call(
            kern,
            grid=(batch, tiles),
            in_specs=[
                pl.BlockSpec((None, DSA_TQ, ATTN_WIDTH),
                             lambda b, i, base=base: (b, base + i, COL_Q // ATTN_WIDTH)),
                pl.BlockSpec((None, DSA_TQ, IDX_HEADS * IDX_DIM),
                             lambda b, i, base=base: (b, base + i, COL_QI // (IDX_HEADS * IDX_DIM))),
                pl.BlockSpec((None, width, LANES), lambda b, i: (b, 0, COL_KV // LANES)),
                pl.BlockSpec((None, width, LANES), lambda b, i: (b, 0, COL_KIW // LANES)),
                pl.BlockSpec((None, DSA_TQ, LANES),
                             lambda b, i, base=base: (b, base + i, COL_KIW // LANES)),
                pl.BlockSpec((LANES, LANES), lambda b, i: (0, 0)),
            ],
            out_specs=pl.BlockSpec((None, DSA_TQ, ATTN_WIDTH), lambda b, i: (b, i, 0)),
            out_shape=jax.ShapeDtypeStruct((batch, class_len, ATTN_WIDTH), BF16),
            scratch_shapes=[
                pltpu.VMEM((width, LANES), BF16),
                pltpu.VMEM((DSA_TQ, width), F32),
                pltpu.VMEM((DSA_TQ, width), F32),
            ],
            compiler_params=_cparams(("arbitrary", "arbitrary")),
            name=f"dsa_attention_w{width}",
        )(proj3, proj3, proj3, proj3, proj3, tri))
    return jnp.concatenate(outs, axis=1)


MERGE_ROWS = 512


def _top2_gates(logits):
    lane = lax.broadcasted_iota(jnp.int32, logits.shape, 1)
    neg_inf = jnp.float32(-jnp.inf)
    lg = jnp.where(lane < N_EXPERTS, logits, neg_inf)
    v1 = jnp.max(lg, axis=-1, keepdims=True)
    i1 = jnp.min(jnp.where(lg == v1, lane, LANES), axis=-1, keepdims=True)
    rest = jnp.where(lane == i1, neg_inf, lg)
    v2 = jnp.max(rest, axis=-1, keepdims=True)
    i2 = jnp.min(jnp.where(rest == v2, lane, LANES), axis=-1, keepdims=True)
    e2 = jnp.exp(v2 - v1)
    den = 1.0 + e2
    return jnp.where(lane == i1, 1.0 / den, 0.0) + jnp.where(lane == i2, e2 / den, 0.0)


def _merge_kernel(*refs, with_router):
    if with_router:
        (at_ref, mab_ref, g2_ref, x_ref, woc_ref, wo_ref, ln_ref, rw_ref,
         x1_ref, h2_ref, gate_ref) = refs
    else:
        at_ref, mab_ref, g2_ref, x_ref, woc_ref, wo_ref, ln_ref, x1_ref, h2_ref = refs
    y_c = _dot(at_ref[...], woc_ref[...])
    mixed = mab_ref[...].astype(F32) + _sigmoid(g2_ref[...].astype(F32)) * y_c
    x1 = x_ref[...] + _dot(mixed.astype(BF16), wo_ref[...])
    x1_ref[...] = x1
    ms = jnp.mean(x1 * x1, axis=-1, keepdims=True)
    h2 = ((x1 * lax.rsqrt(ms + EPS)) * ln_ref[...]).astype(BF16)
    h2_ref[...] = h2
    if with_router:
        gate_ref[...] = _top2_gates(_dot(h2, rw_ref[...]))


def _merge(attn, mab, proj, x2d, w_out_c, w_o, ln2_g, router_w):
    m = x2d.shape[0]
    with_router = router_w is not None
    row = lambda i: (i, 0)
    const = lambda i: (0, 0)
    in_specs = [
        pl.BlockSpec((MERGE_ROWS, ATTN_WIDTH), row),
        pl.BlockSpec((MERGE_ROWS, D_MODEL), row),
        pl.BlockSpec((MERGE_ROWS, D_MODEL), lambda i: (i, COL_GATES // D_MODEL + 2)),
        pl.BlockSpec((MERGE_ROWS, D_MODEL), row),
        pl.BlockSpec((ATTN_WIDTH, D_MODEL), const),
        pl.BlockSpec((D_MODEL, D_MODEL), const),
        pl.BlockSpec((1, D_MODEL), const),
    ]
    out_specs = [pl.BlockSpec((MERGE_ROWS, D_MODEL), row), pl.BlockSpec((MERGE_ROWS, D_MODEL), row)]
    out_shape = [jax.ShapeDtypeStruct((m, D_MODEL), F32), jax.ShapeDtypeStruct((m, D_MODEL), BF16)]
    args = [attn, mab, proj, x2d, w_out_c, w_o, ln2_g]
    if with_router:
        in_specs.append(pl.BlockSpec((D_MODEL, LANES), const))
        out_specs.append(pl.BlockSpec((MERGE_ROWS, LANES), row))
        out_shape.append(jax.ShapeDtypeStruct((m, LANES), F32))
        args.append(router_w)
    return pl.pallas_call(
        functools.partial(_merge_kernel, with_router=with_router),
        grid=(m // MERGE_ROWS,),
        in_specs=in_specs,
        out_specs=out_specs,
        out_shape=out_shape,
        compiler_params=_cparams(("parallel",)),
        name="merge_out_router" if with_router else "merge_out",
    )(*args)


FFN_ROWS = 512


def _ffn_kernel(*refs, gated):
    if gated:
        h_ref, x_ref, gate_ref, w1_ref, w3_ref, w2_ref, o_ref, acc_ref = refs
    else:
        h_ref, x_ref, w1_ref, w3_ref, w2_ref, o_ref = refs
    e = pl.program_id(1)
    f = pl.program_id(2)
    nf = pl.num_programs(2)

    @pl.when((e == 0) & (f == 0))
    def _():
        o_ref[...] = x_ref[...]

    h = h_ref[...]
    a = _dot(h, w1_ref[...])
    b = _dot(h, w3_ref[...])
    part = _dot(((a * _sigmoid(a)) * b).astype(BF16), w2_ref[...])
    if not gated:
        o_ref[...] += part
        return

    @pl.when(f == 0)
    def _():
        acc_ref[...] = part

    @pl.when(f > 0)
    def _():
        acc_ref[...] += part

    @pl.when(f == nf - 1)
    def _():
        lane = lax.broadcasted_iota(jnp.int32, gate_ref.shape, 1)
        g = jnp.sum(jnp.where(lane == e, gate_ref[...], 0.0), axis=-1, keepdims=True)
        o_ref[...] += g * acc_ref[...]


def _ffn(h2, x1, gate, w1, w3, w2, ff_tile):
    m = x1.shape[0]
    n_e, _, d_ff = w1.shape
    gated = gate is not None
    row = lambda i, e, f: (i, 0)
    in_specs = [pl.BlockSpec((FFN_ROWS, D_MODEL), row), pl.BlockSpec((FFN_ROWS, D_MODEL), row)]
    args = [h2, x1]
    scratch = []
    if gated:
        in_specs.append(pl.BlockSpec((FFN_ROWS, LANES), row))
        args.append(gate)
        scratch.append(pltpu.VMEM((FFN_ROWS, D_MODEL), F32))
    in_specs += [
        pl.BlockSpec((None, D_MODEL, ff_tile), lambda i, e, f: (e, 0, f)),
        pl.BlockSpec((None, D_MODEL, ff_tile), lambda i, e, f: (e, 0, f)),
        pl.BlockSpec((None, ff_tile, D_MODEL), lambda i, e, f: (e, f, 0)),
    ]
    args += [w1, w3, w2]
    return pl.pallas_call(
        functools.partial(_ffn_kernel, gated=gated),
        grid=(m // FFN_ROWS, n_e, d_ff // ff_tile),
        in_specs=in_specs,
        out_specs=pl.BlockSpec((FFN_ROWS, D_MODEL), row),
        out_shape=jax.ShapeDtypeStruct((m, D_MODEL), F32),
        scratch_shapes=scratch,
        compiler_params=_cparams(("parallel", "arbitrary", "arbitrary")),
        name="moe_ffn" if gated else "dense_ffn",
    )(*args)


def _pad_in_proj(w_in):
    d = w_in.shape[0]
    c_q_end = 2560
    c_k, c_v_end = 2560, 2688
    c_qi, c_qi_end = 2688, 2944
    c_ki, c_wi_end = 2944, 3012
    pad = jnp.zeros((d, COL_GATES - (COL_KIW + IDX_DIM + IDX_HEADS)), w_in.dtype)
    return jnp.concatenate(
        [w_in[:, :c_q_end], w_in[:, c_qi:c_qi_end], w_in[:, c_k:c_v_end],
         w_in[:, c_ki:c_wi_end], pad, w_in[:, c_wi_end:]], axis=1).astype(BF16)


def _s5_params(lam_re, lam_im, log_dt, b_re, b_im, c_re, c_im):
    lr = jnp.minimum(lam_re.astype(F32), -1e-4)
    li = lam_im.astype(F32)
    dt = jnp.exp(log_dt.astype(F32))[:, None]
    mag = jnp.exp(lr * dt)
    ang = li * dt
    ab_re = mag * jnp.cos(ang)
    ab_im = mag * jnp.sin(ang)
    nr = ab_re - 1.0
    ni = ab_im
    den = lr * lr + li * li
    coef_re = (nr * lr + ni * li) / den
    coef_im = (ni * lr - nr * li) / den
    bf_re = coef_re[:, :, None] * b_re - coef_im[:, :, None] * b_im
    bf_im = coef_re[:, :, None] * b_im + coef_im[:, :, None] * b_re
    gh = SSM_GROUPS // SSM_HALVES
    eye = jnp.eye(gh, dtype=F32)

    def blockdiag_b(w):
        w = w.reshape(SSM_HALVES, gh, SSM_STATE, SSM_GROUP)
        return jnp.einsum('jgph,gk->jghkp', w, eye).reshape(SSM_HALVES, SSM_HALF_CH, SSM_HALF_STATES)

    def blockdiag_c(w):
        w = w.reshape(SSM_HALVES, gh, SSM_GROUP, SSM_STATE)
        return jnp.einsum('jghp,gk->jgpkh', w, eye).reshape(SSM_HALVES, SSM_HALF_STATES, SSM_HALF_CH)

    bc = jnp.concatenate([blockdiag_b(bf_re), blockdiag_b(bf_im)], axis=2).astype(BF16)
    cc = jnp.concatenate([blockdiag_c(c_re.astype(F32)), -blockdiag_c(c_im.astype(F32))],
                         axis=1).astype(BF16)
    a_re8 = jnp.broadcast_to(ab_re.reshape(1, -1), (SUBLANES, SSM_GROUPS * SSM_STATE))
    a_im8 = jnp.broadcast_to(ab_im.reshape(1, -1), (SUBLANES, SSM_GROUPS * SSM_STATE))
    return bc, a_re8, a_im8, cc


def _rope_tables(seq):
    pos = jnp.arange(seq, dtype=F32)
    inv = ROPE_THETA ** (-jnp.arange(HEAD_DIM // 2, dtype=F32) / (HEAD_DIM // 2))
    ang = pos[:, None] * inv[None, :]
    cos, sin = jnp.cos(ang), jnp.sin(ang)
    cos2 = jnp.concatenate([cos, cos], axis=-1)
    sin2 = jnp.concatenate([-sin, sin], axis=-1)
    one, zero = jnp.ones_like(cos2), jnp.zeros_like(sin2)
    tab_a = jnp.concatenate([cos2, cos2, sin2, sin2], axis=-1)
    tab_b = jnp.concatenate([cos2, one, sin2, zero], axis=-1)
    return tab_a, tab_b


def kernel(x, ln1_g, w_in, conv_w, ssm_lam_re, ssm_lam_im, ssm_log_dt, ssm_b_re, ssm_b_im,
           ssm_c_re, ssm_c_im, ssm_d, ssm_w_glu, q_norm_g, k_norm_g, w_out_a, w_out_b, w_out_c,
           w_o, ln2_g, ffn_w1, ffn_w3, ffn_w2, router_w, moe_w1, moe_w3, moe_w2):
    batch, seq, d = x.shape
    assert batch == SUBLANES and d == D_MODEL
    depth = w_in.shape[0]
    m = batch * seq
    tab_a, tab_b = _rope_tables(seq)
    ii = lax.broadcasted_iota(jnp.int32, (LANES, LANES), 0)
    jj = lax.broadcasted_iota(jnp.int32, (LANES, LANES), 1)
    tri = (ii < jj).astype(BF16)
    gsum = ((ii // HEAD_DIM) == (jj // HEAD_DIM)).astype(BF16)
    r_tm = lax.broadcasted_iota(jnp.int32, (MIX_ROWS, MIX_ROWS), 0)
    r_bm = lax.broadcasted_iota(jnp.int32, (MIX_ROWS, MIX_ROWS), 1)
    perm = ((r_tm % SUBLANES) * MIX_STEPS + r_tm // SUBLANES == r_bm).astype(BF16)
    perm_t = perm.T

    xt = x.reshape(m, d)
    for layer in range(depth):
        q_gain2 = jnp.tile(q_norm_g[layer], 2)[None, :]
        k_gain2 = jnp.tile(k_norm_g[layer], 2)[None, :]
        proj = _inproj(xt, ln1_g[layer][None, :], _pad_in_proj(w_in[layer]), tab_a, tab_b,
                       q_gain2, k_gain2, gsum, seq)
        proj3 = proj.reshape(batch, seq, PROJ_COLS)
        bc, a_re8, a_im8, cc = _s5_params(
            ssm_lam_re[layer], ssm_lam_im[layer], ssm_log_dt[layer], ssm_b_re[layer],
            ssm_b_im[layer], ssm_c_re[layer], ssm_c_im[layer])
        mab = _mixer(proj3, conv_w[layer], perm, perm_t, bc, a_re8, a_im8, cc,
                     ssm_d[layer][None, :], ssm_w_glu[layer].astype(BF16),
                     w_out_a[layer].astype(BF16), w_out_b[layer].astype(BF16))
        mab = mab.reshape(m, D_MODEL)
        attn = _dsa(proj3, tri).reshape(m, ATTN_WIDTH)
        j = layer // 2
        if layer % 2 == 0:
            x1, h2 = _merge(attn, mab, proj, xt, w_out_c[layer].astype(BF16),
                            w_o[layer].astype(BF16), ln2_g[layer][None, :], None)
            xt = _ffn(h2, x1, None, ffn_w1[j][None].astype(BF16), ffn_w3[j][None].astype(BF16),
                      ffn_w2[j][None].astype(BF16), ff_tile=1408)
        else:
            rw = jnp.pad(router_w[j], ((0, 0), (0, LANES - N_EXPERTS))).astype(BF16)
            x1, h2, gate = _merge(attn, mab, proj, xt, w_out_c[layer].astype(BF16),
                                  w_o[layer].astype(BF16), ln2_g[layer][None, :], rw)
            xt = _ffn(h2, x1, gate, moe_w1[j].astype(BF16), moe_w3[j].astype(BF16),
                      moe_w2[j].astype(BF16), ff_tile=1792)
    return xt.reshape(batch, seq, d)
```

```python
import functools

import jax
import jax.numpy as jnp
from jax import lax
from jax.experimental import pallas as pl
from jax.experimental.pallas import tpu as pltpu

F32 = jnp.float32
BF16 = jnp.bfloat16

D_MODEL = 1024
CONV_WIDTH = 512
CONV_K = 3
SSM_WIDTH = 512
SSM_GROUP = 16
SSM_GROUPS = 32
SSM_STATE = 64
ATTN_HEADS = 8
HEAD_DIM = 64
ATTN_WIDTH = 512
IDX_HEADS = 4
IDX_DIM = 64
TOPK_MAX = 256
ROPE_THETA = 10000.0
N_EXPERTS = 8
EPS = 1e-6

SUBLANES = 8
LANES = 128
VMEM_LIMIT_BYTES = 56 * 1024 * 1024

COL_CONV = 0
COL_SSM = 1536
COL_Q = 2048
COL_QI = 2560
COL_KV = 2816
COL_KIW = 2944
COL_GATES = 3072
PROJ_COLS = 6144

SSM_HALVES = 2
SSM_HALF_STATES = SSM_GROUPS // SSM_HALVES * SSM_STATE
SSM_HALF_CH = SSM_WIDTH // SSM_HALVES
SSM_COLS = 2 * SSM_GROUPS * SSM_STATE
SCAN_COLS = 512

BISECT_ITERS = 28
BISECT_UNROLL = 4


def _cparams(sem):
    return pltpu.CompilerParams(dimension_semantics=sem, vmem_limit_bytes=VMEM_LIMIT_BYTES)


def _sigmoid(x):
    return 1.0 / (1.0 + jnp.exp(-x))


def _dot(a, b):
    return jnp.dot(a, b, preferred_element_type=F32)


def _dot_nt(a, b):
    return lax.dot_general(a, b, (((1,), (1,)), ((), ())), preferred_element_type=F32)


INPROJ_ROWS = 256
INPROJ_NCHUNK = 512


def _rope_block(x, cos, sin):
    lane = lax.broadcasted_iota(jnp.int32, x.shape, 1)
    first_half = (lane % HEAD_DIM) < (HEAD_DIM // 2)
    swapped = jnp.where(first_half, pltpu.roll(x, LANES - HEAD_DIM // 2, 1),
                        pltpu.roll(x, HEAD_DIM // 2, 1))
    return x * cos + swapped * sin


def _head_inv_rms(x, gsum):
    sq = x * x
    hi = sq.astype(BF16)
    lo = (sq - hi.astype(F32)).astype(BF16)
    ss = _dot(hi, gsum) + _dot(lo, gsum)
    return lax.rsqrt(ss * (1.0 / HEAD_DIM) + EPS)


def _inproj_kernel(x_ref, g_ref, w_ref, taba_ref, tabb_ref, qg_ref, kg_ref, gsum_ref, o_ref):
    x = x_ref[...]
    ms = jnp.mean(x * x, axis=-1, keepdims=True)
    h = ((x * lax.rsqrt(ms + EPS)) * g_ref[...]).astype(BF16)
    gsum = gsum_ref[...]
    cos_a, sin_a = taba_ref[:, 0:LANES], taba_ref[:, LANES:2 * LANES]
    cos_b, sin_b = tabb_ref[:, 0:LANES], tabb_ref[:, LANES:2 * LANES]
    for n in range(PROJ_COLS // INPROJ_NCHUNK):
        c0 = n * INPROJ_NCHUNK
        res = _dot(h, w_ref[:, c0:c0 + INPROJ_NCHUNK])
        if c0 == COL_Q:
            for blk in range(INPROJ_NCHUNK // LANES):
                xb = res[:, blk * LANES:(blk + 1) * LANES]
                xb = (xb * _head_inv_rms(xb, gsum)) * qg_ref[...]
                xb = _rope_block(xb, cos_a, sin_a) * (HEAD_DIM ** -0.5)
                o_ref[:, c0 + blk * LANES:c0 + (blk + 1) * LANES] = xb.astype(o_ref.dtype)
        elif c0 == COL_QI:
            for blk in range(INPROJ_NCHUNK // LANES):
                xb = res[:, blk * LANES:(blk + 1) * LANES]
                col = c0 + blk * LANES
                if col < COL_KV:
                    xb = _rope_block(xb, cos_a, sin_a)
                elif col == COL_KV:
                    lane = lax.broadcasted_iota(jnp.int32, xb.shape, 1)
                    xn = (xb * _head_inv_rms(xb, gsum)) * kg_ref[...]
                    xb = _rope_block(jnp.where(lane < HEAD_DIM, xn, xb), cos_b, sin_b)
                else:
                    xb = _rope_block(xb, cos_b, sin_b)
                o_ref[:, col:col + LANES] = xb.astype(o_ref.dtype)
        else:
            o_ref[:, c0:c0 + INPROJ_NCHUNK] = res.astype(o_ref.dtype)


def _inproj(x2d, ln_g, w_in_p, tab_a, tab_b, q_gain2, k_gain2, gsum, seq):
    m = x2d.shape[0]
    tiles_per_seq = seq // INPROJ_ROWS
    const = lambda i: (0, 0)
    return pl.pallas_call(
        _inproj_kernel,
        grid=(m // INPROJ_ROWS,),
        in_specs=[
            pl.BlockSpec((INPROJ_ROWS, D_MODEL), lambda i: (i, 0)),
            pl.BlockSpec((1, D_MODEL), const),
            pl.BlockSpec((D_MODEL, PROJ_COLS), const),
            pl.BlockSpec((INPROJ_ROWS, 2 * LANES), lambda i: (i % tiles_per_seq, 0)),
            pl.BlockSpec((INPROJ_ROWS, 2 * LANES), lambda i: (i % tiles_per_seq, 0)),
            pl.BlockSpec((1, LANES), const),
            pl.BlockSpec((1, LANES), const),
            pl.BlockSpec((LANES, LANES), const),
        ],
        out_specs=pl.BlockSpec((INPROJ_ROWS, PROJ_COLS), lambda i: (i, 0)),
        out_shape=jax.ShapeDtypeStruct((m, PROJ_COLS), BF16),
        compiler_params=_cparams(("parallel",)),
        name="inproj",
    )(x2d, ln_g, w_in_p, tab_a, tab_b, q_gain2, k_gain2, gsum)


MIX_STEPS = 64
MIX_ROWS = MIX_STEPS * SUBLANES


def _mixer_kernel(pa_ref, su_ref, g0_ref, g1_ref, cw_ref, perm_ref, permt_ref, bc_ref, are_ref,
                  aim_ref, cc_ref, d_ref, wglu_ref, woa_ref, wob_ref, o_ref,
                  xs_ref, hst_ref, vext_ref):
    i = pl.program_id(0)
    rows = MIX_ROWS
    tc = MIX_STEPS

    @pl.when(i == 0)
    def _():
        hst_ref[...] = jnp.zeros_like(hst_ref)
        vext_ref[:, 0:SUBLANES, :] = jnp.zeros((SUBLANES, SUBLANES, CONV_WIDTH), F32)

    u = pa_ref[:, :, 0:CONV_WIDTH].astype(F32)
    bg = pa_ref[:, :, CONV_WIDTH:2 * CONV_WIDTH].astype(F32)
    cg = pa_ref[:, :, 2 * CONV_WIDTH:3 * CONV_WIDTH].astype(F32)
    v = cg * u
    vext_ref[:, SUBLANES:SUBLANES + tc, :] = v
    y = (cw_ref[0:1, :] * vext_ref[:, SUBLANES - 2:SUBLANES - 2 + tc, :]
         + cw_ref[1:2, :] * vext_ref[:, SUBLANES - 1:SUBLANES - 1 + tc, :]
         + cw_ref[2:3, :] * v)
    vext_ref[:, 0:SUBLANES, :] = v[:, tc - SUBLANES:tc, :]
    y_a = _dot((bg * y).reshape(rows, CONV_WIDTH).astype(BF16), woa_ref[...])

    u_tm = _dot(perm_ref[...], su_ref[...].reshape(rows, SSM_WIDTH))
    u_tm_b = u_tm.astype(BF16)
    for j in range(SSM_HALVES):
        xs_ref[:, j * 2 * SSM_HALF_STATES:(j + 1) * 2 * SSM_HALF_STATES] = _dot(
            u_tm_b[:, j * SSM_HALF_CH:(j + 1) * SSM_HALF_CH], bc_ref[j])

    for j in range(SSM_HALVES):
        for q in range(SSM_HALF_STATES // SCAN_COLS):
            cr = j * 2 * SSM_HALF_STATES + q * SCAN_COLS
            ci = cr + SSM_HALF_STATES
            ca = j * SSM_HALF_STATES + q * SCAN_COLS
            ar = are_ref[:, ca:ca + SCAN_COLS]
            ai = aim_ref[:, ca:ca + SCAN_COLS]

            def step(t, carry, cr=cr, ci=ci, ar=ar, ai=ai):
                hr, hi = carry
                r0 = pl.multiple_of(t * SUBLANES, SUBLANES)
                xr = xs_ref[pl.ds(r0, SUBLANES), cr:cr + SCAN_COLS]
                xi = xs_ref[pl.ds(r0, SUBLANES), ci:ci + SCAN_COLS]
                nr = ar * hr - ai * hi + xr
                ni = ar * hi + ai * hr + xi
                xs_ref[pl.ds(r0, SUBLANES), cr:cr + SCAN_COLS] = nr
                xs_ref[pl.ds(r0, SUBLANES), ci:ci + SCAN_COLS] = ni
                return nr, ni

            hr, hi = lax.fori_loop(
                0, MIX_STEPS, step,
                (hst_ref[:, cr:cr + SCAN_COLS], hst_ref[:, ci:ci + SCAN_COLS]), unroll=4)
            hst_ref[:, cr:cr + SCAN_COLS] = hr
            hst_ref[:, ci:ci + SCAN_COLS] = hi

    ys = []
    for j in range(SSM_HALVES):
        hj = xs_ref[:, j * 2 * SSM_HALF_STATES:(j + 1) * 2 * SSM_HALF_STATES].astype(BF16)
        ys.append(_dot(hj, cc_ref[j]))
    ysum = jnp.concatenate(ys, axis=-1) + d_ref[...] * u_tm
    yg = jax.nn.gelu(ysum, approximate=True)
    yg = (yg * _sigmoid(_dot(yg.astype(BF16), wglu_ref[...]))).astype(BF16)
    yg_bm = _dot(permt_ref[...], yg).astype(BF16)
    y_b = _dot(yg_bm, wob_ref[...])

    g0 = g0_ref[...].reshape(rows, D_MODEL).astype(F32)
    g1 = g1_ref[...].reshape(rows, D_MODEL).astype(F32)
    out = _sigmoid(g0) * y_a + _sigmoid(g1) * y_b
    o_ref[...] = out.reshape(SUBLANES, tc, D_MODEL).astype(o_ref.dtype)


def _mixer(proj3, conv_w, perm, perm_t, bc, a_re8, a_im8, cc, d_skip, w_glu, w_out_a, w_out_b):
    batch, seq, _ = proj3.shape
    const2 = lambda i: (0, 0)
    const3 = lambda i: (0, 0, 0)
    return pl.pallas_call(
        _mixer_kernel,
        grid=(seq // MIX_STEPS,),
        in_specs=[
            pl.BlockSpec((batch, MIX_STEPS, 3 * CONV_WIDTH),
                         lambda i: (0, i, COL_CONV // (3 * CONV_WIDTH))),
            pl.BlockSpec((batch, MIX_STEPS, SSM_WIDTH), lambda i: (0, i, COL_SSM // SSM_WIDTH)),
            pl.BlockSpec((batch, MIX_STEPS, D_MODEL), lambda i: (0, i, COL_GATES // D_MODEL)),
            pl.BlockSpec((batch, MIX_STEPS, D_MODEL), lambda i: (0, i, COL_GATES // D_MODEL + 1)),
            pl.BlockSpec((CONV_K, CONV_WIDTH), const2),
            pl.BlockSpec((MIX_ROWS, MIX_ROWS), const2),
            pl.BlockSpec((MIX_ROWS, MIX_ROWS), const2),
            pl.BlockSpec((SSM_HALVES, SSM_HALF_CH, 2 * SSM_HALF_STATES), const3),
            pl.BlockSpec((SUBLANES, SSM_GROUPS * SSM_STATE), const2),
            pl.BlockSpec((SUBLANES, SSM_GROUPS * SSM_STATE), const2),
            pl.BlockSpec((SSM_HALVES, 2 * SSM_HALF_STATES, SSM_HALF_CH), const3),
            pl.BlockSpec((1, SSM_WIDTH), const2),
            pl.BlockSpec((SSM_WIDTH, SSM_WIDTH), const2),
            pl.BlockSpec((CONV_WIDTH, D_MODEL), const2),
            pl.BlockSpec((SSM_WIDTH, D_MODEL), const2),
        ],
        out_specs=pl.BlockSpec((batch, MIX_STEPS, D_MODEL), lambda i: (0, i, 0)),
        out_shape=jax.ShapeDtypeStruct((batch, seq, D_MODEL), BF16),
        scratch_shapes=[
            pltpu.VMEM((MIX_ROWS, SSM_COLS), F32),
            pltpu.VMEM((SUBLANES, SSM_COLS), F32),
            pltpu.VMEM((SUBLANES, MIX_STEPS + SUBLANES, CONV_WIDTH), F32),
        ],
        compiler_params=_cparams(("arbitrary",)),
        name="conv_s5_mixer",
    )(proj3, proj3, proj3, proj3, conv_w, perm, perm_t, bc, a_re8, a_im8, cc, d_skip, w_glu,
      w_out_a, w_out_b)


DSA_TQ = 256
DSA_CLASSES = 4
DSA_CHAINS = 2


def _dsa_kernel(q_ref, qi_ref, kv_ref, kiw_ref, kiwq_ref, tri_ref, o_ref,
                vext_ref, score_ref, bias_ref, *, width, topk, row0):
    i = pl.program_id(1)
    tq = DSA_TQ
    rows_c = tq // DSA_CHAINS
    neg_inf = jnp.float32(-jnp.inf)
    pos_inf = jnp.float32(jnp.inf)

    @pl.when(i == 0)
    def _():
        lane = lax.broadcasted_iota(jnp.int32, (width, LANES), 1)
        shifted = pltpu.roll(kv_ref[...].astype(F32), HEAD_DIM, 1)
        vext_ref[...] = jnp.where(lane < HEAD_DIM, shifted, 1.0).astype(BF16)

    t0 = row0 + i * tq

    wq = kiwq_ref[:, IDX_DIM:IDX_DIM + IDX_HEADS].astype(F32) * (IDX_HEADS ** -0.5)
    kirot = kiw_ref[:, 0:IDX_DIM]
    score = None
    for h in range(IDX_HEADS):
        rel = jnp.maximum(_dot_nt(qi_ref[:, h * IDX_DIM:(h + 1) * IDX_DIM], kirot), 0.0)
        rel = rel * wq[:, h:h + 1]
        score = rel if score is None else score + rel
    col = lax.broadcasted_iota(jnp.int32, (tq, width), 1)
    row = t0 + lax.broadcasted_iota(jnp.int32, (tq, width), 0)
    score_ref[...] = jnp.where(col <= row, score, neg_inf)

    def chain_rows(c):
        return slice(c * rows_c, (c + 1) * rows_c)

    kks, los, his = [], [], []
    for c in range(DSA_CHAINS):
        sc = score_ref[chain_rows(c), :]
        t_row = t0 + c * rows_c + lax.broadcasted_iota(jnp.int32, (rows_c, 1), 0)
        kks.append(jnp.minimum(t_row + 1, topk).astype(F32))
        his.append(jnp.max(sc, axis=-1, keepdims=True))
        los.append(jnp.min(jnp.where(sc == neg_inf, pos_inf, sc), axis=-1, keepdims=True))

    def bisect_round(carry):
        it, los_c, his_c, _ = carry
        open_rows = jnp.float32(0.0)
        for _ in range(BISECT_UNROLL):
            new_lo, new_hi = [], []
            open_rows = jnp.float32(0.0)
            for c in range(DSA_CHAINS):
                sc = score_ref[chain_rows(c), :]
                mid = 0.5 * los_c[c] + 0.5 * his_c[c]
                cnt = jnp.sum(jnp.where(sc >= mid, 1.0, 0.0), axis=-1, keepdims=True)
                ge = cnt >= kks[c]
                eq = cnt == kks[c]
                new_lo.append(jnp.where(ge, mid, los_c[c]))
                new_hi.append(jnp.where(ge & jnp.logical_not(eq), his_c[c], mid))
                open_rows = jnp.maximum(open_rows, jnp.max(jnp.where(eq, 0.0, 1.0)))
            los_c, his_c = tuple(new_lo), tuple(new_hi)
        return it + BISECT_UNROLL, los_c, his_c, open_rows

    def bisect_cond(carry):
        it, _, _, open_rows = carry
        return (it < BISECT_ITERS) & (open_rows > 0.0)

    _, los, _, _ = lax.while_loop(
        bisect_cond, bisect_round, (jnp.int32(0), tuple(los), tuple(his), jnp.float32(1.0)))
    lo = jnp.concatenate(los, axis=0)
    kk = jnp.concatenate(kks, axis=0)

    def count_gt(thr):
        return jnp.sum(jnp.where(score_ref[...] > thr, 1.0, 0.0), axis=-1, keepdims=True)

    def snap_from(lo_v, strict):
        sc = score_ref[...]
        m = (sc > lo_v) if strict else (sc >= lo_v)
        return jnp.min(jnp.where(m, sc, pos_inf), axis=-1, keepdims=True)

    thr0 = snap_from(lo, False)
    above0 = count_gt(thr0)

    def refine_cond(carry):
        _, above = carry
        return jnp.max(jnp.where(above >= kk, 1.0, 0.0)) > 0.0

    def refine_body(carry):
        thr, above = carry
        thr = jnp.where(above >= kk, snap_from(thr, True), thr)
        return thr, count_gt(thr)

    thr, above = lax.while_loop(refine_cond, refine_body, (thr0, above0))

    need = kk - above
    ones_blk = jnp.ones((LANES, LANES), BF16)
    offs = jnp.zeros((tq, LANES), F32)
    for c in range(width // LANES):
        sl = slice(c * LANES, (c + 1) * LANES)
        sc = score_ref[:, sl]
        tie = sc == thr
        tie_b = jnp.where(tie, 1.0, 0.0).astype(BF16)
        before = _dot(tie_b, tri_ref[...]) + offs
        sel = (sc > thr) | (tie & (before < need))
        bias_ref[:, sl] = jnp.where(sel, 0.0, neg_inf)
        offs = offs + _dot(tie_b, ones_blk)

    krot = kv_ref[:, 0:HEAD_DIM]
    vext = vext_ref[...]
    for h in range(ATTN_HEADS):
        lg = _dot_nt(q_ref[:, h * HEAD_DIM:(h + 1) * HEAD_DIM], krot) + bias_ref[...]
        mx = jnp.max(lg, axis=-1, keepdims=True)
        p = jnp.exp(lg - mx).astype(BF16)
        ov = _dot(p, vext)
        oh = ov[:, 0:HEAD_DIM] / ov[:, HEAD_DIM:HEAD_DIM + 1]
        o_ref[:, h * HEAD_DIM:(h + 1) * HEAD_DIM] = oh.astype(o_ref.dtype)


def _dsa(proj3, tri):
    batch, seq, _ = proj3.shape
    topk = min(TOPK_MAX, seq // 4)
    class_len = seq // DSA_CLASSES
    tiles = class_len // DSA_TQ
    outs = []
    for c in range(DSA_CLASSES):
        width = (c + 1) * class_len
        base = c * tiles
        kern = functools.partial(_dsa_kernel, width=width, topk=topk, row0=c * class_len)
        outs.append(pl.pallas_call(
            kern,
            grid=(batch, tiles),
            in_specs=[
                pl.BlockSpec((None, DSA_TQ, ATTN_WIDTH),
                             lambda b, i, base=base: (b, base + i, COL_Q // ATTN_WIDTH)),
                pl.BlockSpec((None, DSA_TQ, IDX_HEADS * IDX_DIM),
                             lambda b, i, base=base: (b, base + i, COL_QI // (IDX_HEADS * IDX_DIM))),
                pl.BlockSpec((None, width, LANES), lambda b, i: (b, 0, COL_KV // LANES)),
                pl.BlockSpec((None, width, LANES), lambda b, i: (b, 0, COL_KIW // LANES)),
                pl.BlockSpec((None, DSA_TQ, LANES),
                             lambda b, i, base=base: (b, base + i, COL_KIW // LANES)),
                pl.BlockSpec((LANES, LANES), lambda b, i: (0, 0)),
            ],
            out_specs=pl.BlockSpec((None, DSA_TQ, ATTN_WIDTH), lambda b, i: (b, i, 0)),
            out_shape=jax.ShapeDtypeStruct((batch, class_len, ATTN_WIDTH), BF16),
            scratch_shapes=[
                pltpu.VMEM((width, LANES), BF16),
                pltpu.VMEM((DSA_TQ, width), F32),
                pltpu.VMEM((DSA_TQ, width), F32),
            ],
            compiler_params=_cparams(("arbitrary", "arbitrary")),
            name=f"dsa_attention_w{width}",
        )(proj3, proj3, proj3, proj3, proj3, tri))
    return jnp.concatenate(outs, axis=1)


MERGE_ROWS = 512


def _top2_gates(logits):
    lane = lax.broadcasted_iota(jnp.int32, logits.shape, 1)
    neg_inf = jnp.float32(-jnp.inf)
    lg = jnp.where(lane < N_EXPERTS, logits, neg_inf)
    v1 = jnp.max(lg, axis=-1, keepdims=True)
    i1 = jnp.min(jnp.where(lg == v1, lane, LANES), axis=-1, keepdims=True)
    rest = jnp.where(lane == i1, neg_inf, lg)
    v2 = jnp.max(rest, axis=-1, keepdims=True)
    i2 = jnp.min(jnp.where(rest == v2, lane, LANES), axis=-1, keepdims=True)
    e2 = jnp.exp(v2 - v1)
    den = 1.0 + e2
    return jnp.where(lane == i1, 1.0 / den, 0.0) + jnp.where(lane == i2, e2 / den, 0.0)


def _merge_kernel(*refs, with_router):
    if with_router:
        (at_ref, mab_ref, g2_ref, x_ref, woc_ref, wo_ref, ln_ref, rw_ref,
         x1_ref, h2_ref, gate_ref) = refs
    else:
        at_ref, mab_ref, g2_ref, x_ref, woc_ref, wo_ref, ln_ref, x1_ref, h2_ref = refs
    y_c = _dot(at_ref[...], woc_ref[...])
    mixed = mab_ref[...].astype(F32) + _sigmoid(g2_ref[...].astype(F32)) * y_c
    x1 = x_ref[...] + _dot(mixed.astype(BF16), wo_ref[...])
    x1_ref[...] = x1
    ms = jnp.mean(x1 * x1, axis=-1, keepdims=True)
    h2 = ((x1 * lax.rsqrt(ms + EPS)) * ln_ref[...]).astype(BF16)
    h2_ref[...] = h2
    if with_router:
        gate_ref[...] = _top2_gates(_dot(h2, rw_ref[...]))


def _merge(attn, mab, proj, x2d, w_out_c, w_o, ln2_g, router_w):
    m = x2d.shape[0]
    with_router = router_w is not None
    row = lambda i: (i, 0)
    const = lambda i: (0, 0)
    in_specs = [
        pl.BlockSpec((MERGE_ROWS, ATTN_WIDTH), row),
        pl.BlockSpec((MERGE_ROWS, D_MODEL), row),
        pl.BlockSpec((MERGE_ROWS, D_MODEL), lambda i: (i, COL_GATES // D_MODEL + 2)),
        pl.BlockSpec((MERGE_ROWS, D_MODEL), row),
        pl.BlockSpec((ATTN_WIDTH, D_MODEL), const),
        pl.BlockSpec((D_MODEL, D_MODEL), const),
        pl.BlockSpec((1, D_MODEL), const),
    ]
    out_specs = [pl.BlockSpec((MERGE_ROWS, D_MODEL), row), pl.BlockSpec((MERGE_ROWS, D_MODEL), row)]
    out_shape = [jax.ShapeDtypeStruct((m, D_MODEL), F32), jax.ShapeDtypeStruct((m, D_MODEL), BF16)]
    args = [attn, mab, proj, x2d, w_out_c, w_o, ln2_g]
    if with_router:
        in_specs.append(pl.BlockSpec((D_MODEL, LANES), const))
        out_specs.append(pl.BlockSpec((MERGE_ROWS, LANES), row))
        out_shape.append(jax.ShapeDtypeStruct((m, LANES), F32))
        args.append(router_w)
    return pl.pallas_call(
        functools.partial(_merge_kernel, with_router=with_router),
        grid=(m // MERGE_ROWS,),
        in_specs=in_specs,
        out_specs=out_specs,
        out_shape=out_shape,
        compiler_params=_cparams(("parallel",)),
        name="merge_out_router" if with_router else "merge_out",
    )(*args)


FFN_ROWS = 512
FFN_TILE = 1408


def _swiglu_part(h, w1, w3, w2):
    a = _dot(h, w1)
    b = _dot(h, w3)
    return _dot(((a * _sigmoid(a)) * b).astype(BF16), w2)


def _ffn_kernel(h_ref, x_ref, w1_ref, w3_ref, w2_ref, o_ref):
    @pl.when(pl.program_id(1) == 0)
    def _():
        o_ref[...] = x_ref[...]

    o_ref[...] += _swiglu_part(h_ref[...], w1_ref[...], w3_ref[...], w2_ref[...])


def _ffn(h2, x1, w1, w3, w2):
    m = x1.shape[0]
    d_ff = w1.shape[1]
    row = lambda i, f: (i, 0)
    return pl.pallas_call(
        _ffn_kernel,
        grid=(m // FFN_ROWS, d_ff // FFN_TILE),
        in_specs=[
            pl.BlockSpec((FFN_ROWS, D_MODEL), row),
            pl.BlockSpec((FFN_ROWS, D_MODEL), row),
            pl.BlockSpec((D_MODEL, FFN_TILE), lambda i, f: (0, f)),
            pl.BlockSpec((D_MODEL, FFN_TILE), lambda i, f: (0, f)),
            pl.BlockSpec((FFN_TILE, D_MODEL), lambda i, f: (f, 0)),
        ],
        out_specs=pl.BlockSpec((FFN_ROWS, D_MODEL), row),
        out_shape=jax.ShapeDtypeStruct((m, D_MODEL), F32),
        compiler_params=_cparams(("parallel", "arbitrary")),
        name="dense_ffn",
    )(h2, x1, w1, w3, w2)


MOE_TILE = 512
MOE_FF_TILE = 1792
MOE_BLK = 256


def _count_le(sorted_ends, v):
    return jnp.sum((sorted_ends[None, :] <= v[:, None]).astype(jnp.int32), axis=1)


def _moe_route(gate):
    m = gate.shape[0]
    i32 = jnp.int32
    sel = gate[:, :N_EXPERTS] > 0.0
    seli = sel.astype(i32)
    csum = jnp.cumsum(seli, axis=0)
    rank = csum - seli
    cnt = csum[-1]
    gsz = (cnt + MOE_TILE - 1) // MOE_TILE * MOE_TILE
    gend = jnp.cumsum(gsz)
    goff = gend - gsz
    n_pos = 2 * m + N_EXPERTS * MOE_TILE
    pos = jnp.where(sel, goff[None, :] + rank, -1).astype(i32)
    tok = lax.broadcasted_iota(i32, (m, N_EXPERTS), 0)
    flat = jnp.where(sel, pos, n_pos + MOE_BLK)
    src = jnp.full((n_pos + MOE_BLK,), -1, i32).at[flat.ravel()].set(tok.ravel(), mode="drop")

    n_ft = n_pos // MOE_TILE
    t0 = jnp.arange(n_ft, dtype=i32) * MOE_TILE
    tile_expert = jnp.minimum(_count_le(gend, t0), N_EXPERTS - 1)
    tile_active = (t0 < gend[-1]).astype(i32)

    n_db = n_pos // MOE_BLK
    srcb = src[:n_pos].reshape(n_db, MOE_BLK)
    first = srcb[:, 0]
    last = jnp.max(srcb, axis=1)
    has = first >= 0
    b_lo = jnp.where(has, first // MOE_BLK, 0)
    n_it = jnp.where(has, last // MOE_BLK - b_lo + 1, 1)
    it_end = jnp.cumsum(n_it)
    it_start = it_end - n_it
    n_items = n_db + N_EXPERTS * (m // MOE_BLK)
    n = jnp.arange(n_items, dtype=i32)
    live = n < it_end[-1]
    d = jnp.minimum(_count_le(it_end, n), n_db - 1)
    item_tile = jnp.where(live, d, n_db).astype(i32)
    item_blk = jnp.where(live, b_lo[d] + n - it_start[d], 0).astype(i32)
    item_first = jnp.where(live, n == it_start[d], True).astype(i32)

    before = jnp.concatenate([jnp.zeros((1, N_EXPERTS), i32), csum], axis=0)[0:m:MOE_BLK]
    blk_a = ((goff[None, :] + before) // MOE_BLK).astype(i32).reshape(-1)
    return dict(pos=pos, src=src[:, None], tile_expert=tile_expert, tile_active=tile_active,
                item_tile=item_tile, item_blk=item_blk, item_first=item_first, blk_a=blk_a,
                n_pos=n_pos)


def _moe_gather_kernel(it_ref, ib_ref, if_ref, src_ref, h_ref, o_ref):
    n = pl.program_id(0)
    rel = src_ref[...] - ib_ref[n] * MOE_BLK
    lane = lax.broadcasted_iota(jnp.int32, (MOE_BLK, MOE_BLK), 1)
    onehot = jnp.where(rel == lane, 1.0, 0.0).astype(BF16)
    res = _dot(onehot, h_ref[...]).astype(o_ref.dtype)

    @pl.when(if_ref[n] == 1)
    def _():
        o_ref[...] = res

    @pl.when(if_ref[n] == 0)
    def _():
        o_ref[...] += res


def _moe_gather(route, h2):
    n_rows = route["n_pos"] + MOE_BLK
    n_items = route["item_tile"].shape[0]
    grid_spec = pltpu.PrefetchScalarGridSpec(
        num_scalar_prefetch=3,
        grid=(n_items,),
        in_specs=[
            pl.BlockSpec((MOE_BLK, 1), lambda n, it, ib, fi: (it[n], 0)),
            pl.BlockSpec((MOE_BLK, D_MODEL), lambda n, it, ib, fi: (ib[n], 0)),
        ],
        out_specs=pl.BlockSpec((MOE_BLK, D_MODEL), lambda n, it, ib, fi: (it[n], 0)),
    )
    return pl.pallas_call(
        _moe_gather_kernel,
        grid_spec=grid_spec,
        out_shape=jax.ShapeDtypeStruct((n_rows, D_MODEL), BF16),
        compiler_params=_cparams(("arbitrary",)),
        name="moe_gather",
    )(route["item_tile"], route["item_blk"], route["item_first"], route["src"], h2)


def _moe_ffn_kernel(te_ref, ta_ref, x_ref, w1_ref, w3_ref, w2_ref, o_ref, acc_ref):
    k = pl.program_id(0)
    f = pl.program_id(1)
    nf = pl.num_programs(1)
    active = ta_ref[k] == 1

    @pl.when(active)
    def _():
        part = _swiglu_part(x_ref[...], w1_ref[...], w3_ref[...], w2_ref[...])

        @pl.when(f == 0)
        def _():
            acc_ref[...] = part

        @pl.when(f > 0)
        def _():
            acc_ref[...] += part

        @pl.when(f == nf - 1)
        def _():
            o_ref[...] = acc_ref[...].astype(o_ref.dtype)

    @pl.when(jnp.logical_not(active) & (f == nf - 1))
    def _():
        o_ref[...] = jnp.zeros_like(o_ref)


def _moe_ffn(route, xg, w1, w3, w2):
    n_pos = route["n_pos"]
    d_ff = w1.shape[2]
    grid_spec = pltpu.PrefetchScalarGridSpec(
        num_scalar_prefetch=2,
        grid=(n_pos // MOE_TILE, d_ff // MOE_FF_TILE),
        in_specs=[
            pl.BlockSpec((MOE_TILE, D_MODEL), lambda k, f, te, ta: (k, 0)),
            pl.BlockSpec((None, D_MODEL, MOE_FF_TILE), lambda k, f, te, ta: (te[k], 0, f)),
            pl.BlockSpec((None, D_MODEL, MOE_FF_TILE), lambda k, f, te, ta: (te[k], 0, f)),
            pl.BlockSpec((None, MOE_FF_TILE, D_MODEL), lambda k, f, te, ta: (te[k], f, 0)),
        ],
        out_specs=pl.BlockSpec((MOE_TILE, D_MODEL), lambda k, f, te, ta: (k, 0)),
        scratch_shapes=[pltpu.VMEM((MOE_TILE, D_MODEL), F32)],
    )
    return pl.pallas_call(
        _moe_ffn_kernel,
        grid_spec=grid_spec,
        out_shape=jax.ShapeDtypeStruct((n_pos, D_MODEL), BF16),
        compiler_params=_cparams(("arbitrary", "arbitrary")),
        name="moe_expert_ffn",
    )(route["tile_expert"], route["tile_active"], xg, w1, w3, w2)


def _moe_combine_kernel(ba_ref, x_ref, gate_ref, pos_ref, *refs):
    y_refs, o_ref = refs[:-1], refs[-1]
    tb = pl.program_id(0)
    lane = lax.broadcasted_iota(jnp.int32, (MOE_BLK, MOE_BLK), 1)
    out = x_ref[...]
    for e in range(N_EXPERTS):
        base = ba_ref[tb * N_EXPERTS + e] * MOE_BLK
        rel = pos_ref[:, e:e + 1] - base
        picked = None
        for half in range(2):
            onehot = jnp.where(rel - half * MOE_BLK == lane, 1.0, 0.0).astype(BF16)
            part = _dot(onehot, y_refs[2 * e + half][...])
            picked = part if picked is None else picked + part
        out = out + gate_ref[:, e:e + 1] * picked
    o_ref[...] = out


def _moe_combine(route, x1, gate, y):
    m = x1.shape[0]
    last_blk = route["n_pos"] // MOE_BLK - 1
    row = lambda tb, ba: (tb, 0)
    y_specs = []
    for e in range(N_EXPERTS):
        for half in range(2):
            y_specs.append(pl.BlockSpec(
                (MOE_BLK, D_MODEL),
                lambda tb, ba, e=e, half=half: (
                    jnp.minimum(ba[tb * N_EXPERTS + e] + half, last_blk), 0)))
    grid_spec = pltpu.PrefetchScalarGridSpec(
        num_scalar_prefetch=1,
        grid=(m // MOE_BLK,),
        in_specs=[
            pl.BlockSpec((MOE_BLK, D_MODEL), row),
            pl.BlockSpec((MOE_BLK, LANES), row),
            pl.BlockSpec((MOE_BLK, N_EXPERTS), row),
        ] + y_specs,
        out_specs=pl.BlockSpec((MOE_BLK, D_MODEL), row),
    )
    return pl.pallas_call(
        _moe_combine_kernel,
        grid_spec=grid_spec,
        out_shape=jax.ShapeDtypeStruct((m, D_MODEL), F32),
        compiler_params=_cparams(("arbitrary",)),
        name="moe_combine",
    )(route["blk_a"], x1, gate, route["pos"], *([y] * (2 * N_EXPERTS)))


def _moe(h2, x1, gate, w1, w3, w2):
    route = _moe_route(gate)
    xg = _moe_gather(route, h2)
    y = _moe_ffn(route, xg, w1, w3, w2)
    return _moe_combine(route, x1, gate, y)


def _pad_in_proj(w_in):
    c_q_end = 2560
    c_k, c_v_end = 2560, 2688
    c_qi, c_qi_end = 2688, 2944
    c_ki, c_wi_end = 2944, 3012
    c_kiw_end = c_ki + LANES
    w = w_in.astype(BF16)
    return jnp.concatenate(
        [w[:, :c_q_end], w[:, c_qi:c_qi_end], w[:, c_k:c_v_end], w[:, c_ki:c_kiw_end],
         w[:, c_wi_end:]], axis=1)


def _s5_params(lam_re, lam_im, log_dt, b_re, b_im, c_re, c_im):
    lr = jnp.minimum(lam_re.astype(F32), -1e-4)
    li = lam_im.astype(F32)
    dt = jnp.exp(log_dt.astype(F32))[:, None]
    mag = jnp.exp(lr * dt)
    ang = li * dt
    ab_re = mag * jnp.cos(ang)
    ab_im = mag * jnp.sin(ang)
    nr = ab_re - 1.0
    ni = ab_im
    den = lr * lr + li * li
    coef_re = (nr * lr + ni * li) / den
    coef_im = (ni * lr - nr * li) / den
    bf_re = coef_re[:, :, None] * b_re - coef_im[:, :, None] * b_im
    bf_im = coef_re[:, :, None] * b_im + coef_im[:, :, None] * b_re
    gh = SSM_GROUPS // SSM_HALVES
    eye = jnp.eye(gh, dtype=F32)

    def blockdiag_b(w):
        w = w.reshape(SSM_HALVES, gh, SSM_STATE, SSM_GROUP)
        return jnp.einsum('jgph,gk->jghkp', w, eye).reshape(SSM_HALVES, SSM_HALF_CH, SSM_HALF_STATES)

    def blockdiag_c(w):
        w = w.reshape(SSM_HALVES, gh, SSM_GROUP, SSM_STATE)
        return jnp.einsum('jghp,gk->jgpkh', w, eye).reshape(SSM_HALVES, SSM_HALF_STATES, SSM_HALF_CH)

    bc = jnp.concatenate([blockdiag_b(bf_re), blockdiag_b(bf_im)], axis=2).astype(BF16)
    cc = jnp.concatenate([blockdiag_c(c_re.astype(F32)), -blockdiag_c(c_im.astype(F32))],
                         axis=1).astype(BF16)
    a_re8 = jnp.broadcast_to(ab_re.reshape(1, -1), (SUBLANES, SSM_GROUPS * SSM_STATE))
    a_im8 = jnp.broadcast_to(ab_im.reshape(1, -1), (SUBLANES, SSM_GROUPS * SSM_STATE))
    return bc, a_re8, a_im8, cc


def _rope_tables(seq):
    pos = jnp.arange(seq, dtype=F32)
    inv = ROPE_THETA ** (-jnp.arange(HEAD_DIM // 2, dtype=F32) / (HEAD_DIM // 2))
    ang = pos[:, None] * inv[None, :]
    cos, sin = jnp.cos(ang), jnp.sin(ang)
    cos2 = jnp.concatenate([cos, cos], axis=-1)
    sin2 = jnp.concatenate([-sin, sin], axis=-1)
    one, zero = jnp.ones_like(cos2), jnp.zeros_like(sin2)
    tab_a = jnp.concatenate([cos2, cos2, sin2, sin2], axis=-1)
    tab_b = jnp.concatenate([cos2, one, sin2, zero], axis=-1)
    return tab_a, tab_b


def kernel(x, ln1_g, w_in, conv_w, ssm_lam_re, ssm_lam_im, ssm_log_dt, ssm_b_re, ssm_b_im,
           ssm_c_re, ssm_c_im, ssm_d, ssm_w_glu, q_norm_g, k_norm_g, w_out_a, w_out_b, w_out_c,
           w_o, ln2_g, ffn_w1, ffn_w3, ffn_w2, router_w, moe_w1, moe_w3, moe_w2):
    batch, seq, d = x.shape
    assert batch == SUBLANES and d == D_MODEL
    depth = w_in.shape[0]
    m = batch * seq
    tab_a, tab_b = _rope_tables(seq)
    ii = lax.broadcasted_iota(jnp.int32, (LANES, LANES), 0)
    jj = lax.broadcasted_iota(jnp.int32, (LANES, LANES), 1)
    tri = (ii < jj).astype(BF16)
    gsum = ((ii // HEAD_DIM) == (jj // HEAD_DIM)).astype(BF16)
    r_tm = lax.broadcasted_iota(jnp.int32, (MIX_ROWS, MIX_ROWS), 0)
    r_bm = lax.broadcasted_iota(jnp.int32, (MIX_ROWS, MIX_ROWS), 1)
    perm = ((r_tm % SUBLANES) * MIX_STEPS + r_tm // SUBLANES == r_bm).astype(BF16)
    perm_t = perm.T

    xt = x.reshape(m, d)
    for layer in range(depth):
        q_gain2 = jnp.tile(q_norm_g[layer], 2)[None, :]
        k_gain2 = jnp.tile(k_norm_g[layer], 2)[None, :]
        proj = _inproj(xt, ln1_g[layer][None, :], _pad_in_proj(w_in[layer]), tab_a, tab_b,
                       q_gain2, k_gain2, gsum, seq)
        proj3 = proj.reshape(batch, seq, PROJ_COLS)
        bc, a_re8, a_im8, cc = _s5_params(
            ssm_lam_re[layer], ssm_lam_im[layer], ssm_log_dt[layer], ssm_b_re[layer],
            ssm_b_im[layer], ssm_c_re[layer], ssm_c_im[layer])
        mab = _mixer(proj3, conv_w[layer], perm, perm_t, bc, a_re8, a_im8, cc,
                     ssm_d[layer][None, :], ssm_w_glu[layer].astype(BF16),
                     w_out_a[layer].astype(BF16), w_out_b[layer].astype(BF16))
        mab = mab.reshape(m, D_MODEL)
        attn = _dsa(proj3, tri).reshape(m, ATTN_WIDTH)
        j = layer // 2
        if layer % 2 == 0:
            x1, h2 = _merge(attn, mab, proj, xt, w_out_c[layer].astype(BF16),
                            w_o[layer].astype(BF16), ln2_g[layer][None, :], None)
            xt = _ffn(h2, x1, ffn_w1[j].astype(BF16), ffn_w3[j].astype(BF16),
                      ffn_w2[j].astype(BF16))
        else:
            rw = jnp.pad(router_w[j], ((0, 0), (0, LANES - N_EXPERTS))).astype(BF16)
            x1, h2, gate = _merge(attn, mab, proj, xt, w_out_c[layer].astype(BF16),
                                  w_o[layer].astype(BF16), ln2_g[layer][None, :], rw)
            xt = _moe(h2, x1, gate, moe_w1[j].astype(BF16), moe_w3[j].astype(BF16),
                      moe_w2[j].astype(BF16))
    return xt.reshape(batch, seq, d)
```

```python
import functools

import jax
import jax.numpy as jnp
from jax import lax
from jax.experimental import pallas as pl
from jax.experimental.pallas import tpu as pltpu

F32 = jnp.float32
BF16 = jnp.bfloat16

D_MODEL = 1024
CONV_WIDTH = 512
CONV_K = 3
SSM_WIDTH = 512
SSM_GROUP = 16
SSM_GROUPS = 32
SSM_STATE = 64
ATTN_HEADS = 8
HEAD_DIM = 64
ATTN_WIDTH = 512
IDX_HEADS = 4
IDX_DIM = 64
TOPK_MAX = 256
ROPE_THETA = 10000.0
N_EXPERTS = 8
EPS = 1e-6

SUBLANES = 8
LANES = 128
VMEM_LIMIT_BYTES = 56 * 1024 * 1024

COL_CONV = 0
COL_SSM = 1536
COL_Q = 2048
COL_QI = 2560
COL_KV = 2816
COL_KIW = 2944
COL_GATES = 3072
PROJ_COLS = 6144

SSM_HALVES = 2
SSM_HALF_STATES = SSM_GROUPS // SSM_HALVES * SSM_STATE
SSM_HALF_CH = SSM_WIDTH // SSM_HALVES
SSM_COLS = 2 * SSM_GROUPS * SSM_STATE
SCAN_COLS = 512

BISECT_ITERS = 28
BISECT_UNROLL = 4


def _cparams(sem):
    return pltpu.CompilerParams(dimension_semantics=sem, vmem_limit_bytes=VMEM_LIMIT_BYTES)


def _sigmoid(x):
    return 1.0 / (1.0 + jnp.exp(-x))


def _dot(a, b):
    return jnp.dot(a, b, preferred_element_type=F32)


def _dot_nt(a, b):
    return lax.dot_general(a, b, (((1,), (1,)), ((), ())), preferred_element_type=F32)


INPROJ_ROWS = 256
INPROJ_NCHUNK = 512


def _rope_block(x, cos, sin):
    lane = lax.broadcasted_iota(jnp.int32, x.shape, 1)
    first_half = (lane % HEAD_DIM) < (HEAD_DIM // 2)
    swapped = jnp.where(first_half, pltpu.roll(x, LANES - HEAD_DIM // 2, 1),
                        pltpu.roll(x, HEAD_DIM // 2, 1))
    return x * cos + swapped * sin


def _head_inv_rms(x, gsum):
    sq = x * x
    hi = sq.astype(BF16)
    lo = (sq - hi.astype(F32)).astype(BF16)
    ss = _dot(hi, gsum) + _dot(lo, gsum)
    return lax.rsqrt(ss * (1.0 / HEAD_DIM) + EPS)


def _inproj_kernel(x_ref, g_ref, w_ref, taba_ref, tabb_ref, qg_ref, kg_ref, gsum_ref, o_ref):
    x = x_ref[...]
    ms = jnp.mean(x * x, axis=-1, keepdims=True)
    h = ((x * lax.rsqrt(ms + EPS)) * g_ref[...]).astype(BF16)
    gsum = gsum_ref[...]
    cos_a, sin_a = taba_ref[:, 0:LANES], taba_ref[:, LANES:2 * LANES]
    cos_b, sin_b = tabb_ref[:, 0:LANES], tabb_ref[:, LANES:2 * LANES]
    for n in range(PROJ_COLS // INPROJ_NCHUNK):
        c0 = n * INPROJ_NCHUNK
        res = _dot(h, w_ref[:, c0:c0 + INPROJ_NCHUNK])
        if c0 == COL_Q:
            for blk in range(INPROJ_NCHUNK // LANES):
                xb = res[:, blk * LANES:(blk + 1) * LANES]
                xb = (xb * _head_inv_rms(xb, gsum)) * qg_ref[...]
                xb = _rope_block(xb, cos_a, sin_a) * (HEAD_DIM ** -0.5)
                o_ref[:, c0 + blk * LANES:c0 + (blk + 1) * LANES] = xb.astype(o_ref.dtype)
        elif c0 == COL_QI:
            for blk in range(INPROJ_NCHUNK // LANES):
                xb = res[:, blk * LANES:(blk + 1) * LANES]
                col = c0 + blk * LANES
                if col < COL_KV:
                    xb = _rope_block(xb, cos_a, sin_a)
                elif col == COL_KV:
                    lane = lax.broadcasted_iota(jnp.int32, xb.shape, 1)
                    xn = (xb * _head_inv_rms(xb, gsum)) * kg_ref[...]
                    xb = _rope_block(jnp.where(lane < HEAD_DIM, xn, xb), cos_b, sin_b)
                else:
                    xb = _rope_block(xb, cos_b, sin_b)
                o_ref[:, col:col + LANES] = xb.astype(o_ref.dtype)
        else:
            o_ref[:, c0:c0 + INPROJ_NCHUNK] = res.astype(o_ref.dtype)


def _inproj(x2d, ln_g, w_in_p, tab_a, tab_b, q_gain2, k_gain2, gsum, seq):
    m = x2d.shape[0]
    tiles_per_seq = seq // INPROJ_ROWS
    const = lambda i: (0, 0)
    return pl.pallas_call(
        _inproj_kernel,
        grid=(m // INPROJ_ROWS,),
        in_specs=[
            pl.BlockSpec((INPROJ_ROWS, D_MODEL), lambda i: (i, 0)),
            pl.BlockSpec((1, D_MODEL), const),
            pl.BlockSpec((D_MODEL, PROJ_COLS), const),
            pl.BlockSpec((INPROJ_ROWS, 2 * LANES), lambda i: (i % tiles_per_seq, 0)),
            pl.BlockSpec((INPROJ_ROWS, 2 * LANES), lambda i: (i % tiles_per_seq, 0)),
            pl.BlockSpec((1, LANES), const),
            pl.BlockSpec((1, LANES), const),
            pl.BlockSpec((LANES, LANES), const),
        ],
        out_specs=pl.BlockSpec((INPROJ_ROWS, PROJ_COLS), lambda i: (i, 0)),
        out_shape=jax.ShapeDtypeStruct((m, PROJ_COLS), BF16),
        compiler_params=_cparams(("parallel",)),
        name="inproj",
    )(x2d, ln_g, w_in_p, tab_a, tab_b, q_gain2, k_gain2, gsum)


MIX_STEPS = 64
MIX_ROWS = MIX_STEPS * SUBLANES


def _mixer_kernel(pa_ref, su_ref, g0_ref, g1_ref, cw_ref, perm_ref, permt_ref, bc_ref, are_ref,
                  aim_ref, cc_ref, d_ref, wglu_ref, woa_ref, wob_ref, o_ref,
                  xs_ref, hst_ref, vext_ref):
    i = pl.program_id(0)
    rows = MIX_ROWS
    tc = MIX_STEPS

    @pl.when(i == 0)
    def _():
        hst_ref[...] = jnp.zeros_like(hst_ref)
        vext_ref[:, 0:SUBLANES, :] = jnp.zeros((SUBLANES, SUBLANES, CONV_WIDTH), F32)

    u = pa_ref[:, :, 0:CONV_WIDTH].astype(F32)
    bg = pa_ref[:, :, CONV_WIDTH:2 * CONV_WIDTH].astype(F32)
    cg = pa_ref[:, :, 2 * CONV_WIDTH:3 * CONV_WIDTH].astype(F32)
    v = cg * u
    vext_ref[:, SUBLANES:SUBLANES + tc, :] = v
    y = (cw_ref[0:1, :] * vext_ref[:, SUBLANES - 2:SUBLANES - 2 + tc, :]
         + cw_ref[1:2, :] * vext_ref[:, SUBLANES - 1:SUBLANES - 1 + tc, :]
         + cw_ref[2:3, :] * v)
    vext_ref[:, 0:SUBLANES, :] = v[:, tc - SUBLANES:tc, :]
    y_a = _dot((bg * y).reshape(rows, CONV_WIDTH).astype(BF16), woa_ref[...])

    u_tm = _dot(perm_ref[...], su_ref[...].reshape(rows, SSM_WIDTH))
    u_tm_b = u_tm.astype(BF16)
    for j in range(SSM_HALVES):
        xs_ref[:, j * 2 * SSM_HALF_STATES:(j + 1) * 2 * SSM_HALF_STATES] = _dot(
            u_tm_b[:, j * SSM_HALF_CH:(j + 1) * SSM_HALF_CH], bc_ref[j])

    for j in range(SSM_HALVES):
        for q in range(SSM_HALF_STATES // SCAN_COLS):
            cr = j * 2 * SSM_HALF_STATES + q * SCAN_COLS
            ci = cr + SSM_HALF_STATES
            ca = j * SSM_HALF_STATES + q * SCAN_COLS
            ar = are_ref[:, ca:ca + SCAN_COLS]
            ai = aim_ref[:, ca:ca + SCAN_COLS]

            def step(t, carry, cr=cr, ci=ci, ar=ar, ai=ai):
                hr, hi = carry
                r0 = pl.multiple_of(t * SUBLANES, SUBLANES)
                xr = xs_ref[pl.ds(r0, SUBLANES), cr:cr + SCAN_COLS]
                xi = xs_ref[pl.ds(r0, SUBLANES), ci:ci + SCAN_COLS]
                nr = ar * hr - ai * hi + xr
                ni = ar * hi + ai * hr + xi
                xs_ref[pl.ds(r0, SUBLANES), cr:cr + SCAN_COLS] = nr
                xs_ref[pl.ds(r0, SUBLANES), ci:ci + SCAN_COLS] = ni
                return nr, ni

            hr, hi = lax.fori_loop(
                0, MIX_STEPS, step,
                (hst_ref[:, cr:cr + SCAN_COLS], hst_ref[:, ci:ci + SCAN_COLS]), unroll=4)
            hst_ref[:, cr:cr + SCAN_COLS] = hr
            hst_ref[:, ci:ci + SCAN_COLS] = hi

    ys = []
    for j in range(SSM_HALVES):
        hj = xs_ref[:, j * 2 * SSM_HALF_STATES:(j + 1) * 2 * SSM_HALF_STATES].astype(BF16)
        ys.append(_dot(hj, cc_ref[j]))
    ysum = jnp.concatenate(ys, axis=-1) + d_ref[...] * u_tm
    yg = jax.nn.gelu(ysum, approximate=True)
    yg = (yg * _sigmoid(_dot(yg.astype(BF16), wglu_ref[...]))).astype(BF16)
    yg_bm = _dot(permt_ref[...], yg).astype(BF16)
    y_b = _dot(yg_bm, wob_ref[...])

    g0 = g0_ref[...].reshape(rows, D_MODEL).astype(F32)
    g1 = g1_ref[...].reshape(rows, D_MODEL).astype(F32)
    out = _sigmoid(g0) * y_a + _sigmoid(g1) * y_b
    o_ref[...] = out.reshape(SUBLANES, tc, D_MODEL).astype(o_ref.dtype)


def _mixer(proj3, conv_w, perm, perm_t, bc, a_re8, a_im8, cc, d_skip, w_glu, w_out_a, w_out_b):
    batch, seq, _ = proj3.shape
    const2 = lambda i: (0, 0)
    const3 = lambda i: (0, 0, 0)
    return pl.pallas_call(
        _mixer_kernel,
        grid=(seq // MIX_STEPS,),
        in_specs=[
            pl.BlockSpec((batch, MIX_STEPS, 3 * CONV_WIDTH),
                         lambda i: (0, i, COL_CONV // (3 * CONV_WIDTH))),
            pl.BlockSpec((batch, MIX_STEPS, SSM_WIDTH), lambda i: (0, i, COL_SSM // SSM_WIDTH)),
            pl.BlockSpec((batch, MIX_STEPS, D_MODEL), lambda i: (0, i, COL_GATES // D_MODEL)),
            pl.BlockSpec((batch, MIX_STEPS, D_MODEL), lambda i: (0, i, COL_GATES // D_MODEL + 1)),
            pl.BlockSpec((CONV_K, CONV_WIDTH), const2),
            pl.BlockSpec((MIX_ROWS, MIX_ROWS), const2),
            pl.BlockSpec((MIX_ROWS, MIX_ROWS), const2),
            pl.BlockSpec((SSM_HALVES, SSM_HALF_CH, 2 * SSM_HALF_STATES), const3),
            pl.BlockSpec((SUBLANES, SSM_GROUPS * SSM_STATE), const2),
            pl.BlockSpec((SUBLANES, SSM_GROUPS * SSM_STATE), const2),
            pl.BlockSpec((SSM_HALVES, 2 * SSM_HALF_STATES, SSM_HALF_CH), const3),
            pl.BlockSpec((1, SSM_WIDTH), const2),
            pl.BlockSpec((SSM_WIDTH, SSM_WIDTH), const2),
            pl.BlockSpec((CONV_WIDTH, D_MODEL), const2),
            pl.BlockSpec((SSM_WIDTH, D_MODEL), const2),
        ],
        out_specs=pl.BlockSpec((batch, MIX_STEPS, D_MODEL), lambda i: (0, i, 0)),
        out_shape=jax.ShapeDtypeStruct((batch, seq, D_MODEL), BF16),
        scratch_shapes=[
            pltpu.VMEM((MIX_ROWS, SSM_COLS), F32),
            pltpu.VMEM((SUBLANES, SSM_COLS), F32),
            pltpu.VMEM((SUBLANES, MIX_STEPS + SUBLANES, CONV_WIDTH), F32),
        ],
        compiler_params=_cparams(("arbitrary",)),
        name="conv_s5_mixer",
    )(proj3, proj3, proj3, proj3, conv_w, perm, perm_t, bc, a_re8, a_im8, cc, d_skip, w_glu,
      w_out_a, w_out_b)


DSA_TQ = 256
DSA_CLASSES = 4
DSA_CHAINS = 2
DSA_HEAD_GROUP = 4


def _dsa_kernel(q_ref, qi_ref, kv_ref, kiw_ref, kiwq_ref, tri_ref, o_ref,
                vext_ref, score_ref, bias_ref, *, width, topk, row0):
    i = pl.program_id(1)
    tq = DSA_TQ
    rows_c = tq // DSA_CHAINS
    neg_inf = jnp.float32(-jnp.inf)
    pos_inf = jnp.float32(jnp.inf)

    @pl.when(i == 0)
    def _():
        lane = lax.broadcasted_iota(jnp.int32, (width, LANES), 1)
        shifted = pltpu.roll(kv_ref[...].astype(F32), HEAD_DIM, 1)
        vext_ref[...] = jnp.where(lane < HEAD_DIM, shifted, 1.0).astype(BF16)

    t0 = row0 + i * tq

    wq = kiwq_ref[:, IDX_DIM:IDX_DIM + IDX_HEADS].astype(F32) * (IDX_HEADS ** -0.5)
    kirot = kiw_ref[:, 0:IDX_DIM]
    score = None
    for h in range(IDX_HEADS):
        rel = jnp.maximum(_dot_nt(qi_ref[:, h * IDX_DIM:(h + 1) * IDX_DIM], kirot), 0.0)
        rel = rel * wq[:, h:h + 1]
        score = rel if score is None else score + rel
    col = lax.broadcasted_iota(jnp.int32, (tq, width), 1)
    row = t0 + lax.broadcasted_iota(jnp.int32, (tq, width), 0)
    score_ref[...] = jnp.where(col <= row, score, neg_inf)

    def chain_rows(c):
        return slice(c * rows_c, (c + 1) * rows_c)

    kks, los, his, fins = [], [], [], []
    for c in range(DSA_CHAINS):
        sc = score_ref[chain_rows(c), :]
        t_row = t0 + c * rows_c + lax.broadcasted_iota(jnp.int32, (rows_c, 1), 0)
        kk_c = jnp.minimum(t_row + 1, topk).astype(F32)
        hi_c = jnp.max(sc, axis=-1, keepdims=True)
        lo_c = jnp.min(jnp.where(sc == neg_inf, pos_inf, sc), axis=-1, keepdims=True)
        n_ge0 = jnp.sum(jnp.where(sc >= 0.0, 1.0, 0.0), axis=-1, keepdims=True)
        n_gt0 = jnp.sum(jnp.where(sc > 0.0, 1.0, 0.0), axis=-1, keepdims=True)
        above0 = n_gt0 >= kk_c
        reach0 = n_ge0 >= kk_c
        kks.append(kk_c)
        los.append(jnp.where(reach0, 0.0, lo_c))
        his.append(jnp.where(above0, hi_c, 0.0))
        fins.append(jnp.where(reach0 & jnp.logical_not(above0), 1.0, 0.0))

    def bisect_round(carry):
        it, los_c, his_c, fins_c, _ = carry
        open_rows = jnp.float32(0.0)
        for _ in range(BISECT_UNROLL):
            new_lo, new_hi, new_fin = [], [], []
            open_rows = jnp.float32(0.0)
            for c in range(DSA_CHAINS):
                sc = score_ref[chain_rows(c), :]
                mid = 0.5 * los_c[c] + 0.5 * his_c[c]
                cnt = jnp.sum(jnp.where(sc >= mid, 1.0, 0.0), axis=-1, keepdims=True)
                live = fins_c[c] == 0.0
                ge = cnt >= kks[c]
                hit = live & (cnt == kks[c])
                new_lo.append(jnp.where(live & ge, mid, los_c[c]))
                new_hi.append(jnp.where(live & (hit | jnp.logical_not(ge)), mid, his_c[c]))
                fin = jnp.where(hit, 1.0, fins_c[c])
                new_fin.append(fin)
                open_rows = jnp.maximum(open_rows, jnp.max(1.0 - fin))
            los_c, his_c, fins_c = tuple(new_lo), tuple(new_hi), tuple(new_fin)
        return it + BISECT_UNROLL, los_c, his_c, fins_c, open_rows

    def bisect_cond(carry):
        return (carry[0] < BISECT_ITERS) & (carry[4] > 0.0)

    _, los, _, _, _ = lax.while_loop(
        bisect_cond, bisect_round,
        (jnp.int32(0), tuple(los), tuple(his), tuple(fins), jnp.float32(1.0)))
    lo = jnp.concatenate(los, axis=0)
    kk = jnp.concatenate(kks, axis=0)

    def count_gt(thr):
        return jnp.sum(jnp.where(score_ref[...] > thr, 1.0, 0.0), axis=-1, keepdims=True)

    def snap_from(lo_v, strict):
        sc = score_ref[...]
        m = (sc > lo_v) if strict else (sc >= lo_v)
        return jnp.min(jnp.where(m, sc, pos_inf), axis=-1, keepdims=True)

    thr0 = snap_from(lo, False)
    above0 = count_gt(thr0)

    def refine_cond(carry):
        _, above = carry
        return jnp.max(jnp.where(above >= kk, 1.0, 0.0)) > 0.0

    def refine_body(carry):
        thr, above = carry
        thr = jnp.where(above >= kk, snap_from(thr, True), thr)
        return thr, count_gt(thr)

    thr, above = lax.while_loop(refine_cond, refine_body, (thr0, above0))

    need = kk - above
    ones_blk = jnp.ones((LANES, LANES), BF16)
    offs = jnp.zeros((tq, LANES), F32)
    for c in range(width // LANES):
        sl = slice(c * LANES, (c + 1) * LANES)
        sc = score_ref[:, sl]
        tie = sc == thr
        tie_b = jnp.where(tie, 1.0, 0.0).astype(BF16)
        before = _dot(tie_b, tri_ref[...]) + offs
        sel = (sc > thr) | (tie & (before < need))
        bias_ref[:, sl] = jnp.where(sel, 0.0, neg_inf)
        offs = offs + _dot(tie_b, ones_blk)

    krot = kv_ref[:, 0:HEAD_DIM]
    vext = vext_ref[...]
    for h0 in range(0, ATTN_HEADS, DSA_HEAD_GROUP):
        heads = range(h0, h0 + DSA_HEAD_GROUP)
        lgs = [_dot_nt(q_ref[:, h * HEAD_DIM:(h + 1) * HEAD_DIM], krot) + bias_ref[...]
               for h in heads]
        mxs = [jnp.max(lg, axis=-1, keepdims=True) for lg in lgs]
        ps = [jnp.exp(lg - mx).astype(BF16) for lg, mx in zip(lgs, mxs)]
        ovs = [_dot(p, vext) for p in ps]
        for h, ov in zip(heads, ovs):
            oh = ov[:, 0:HEAD_DIM] / ov[:, HEAD_DIM:HEAD_DIM + 1]
            o_ref[:, h * HEAD_DIM:(h + 1) * HEAD_DIM] = oh.astype(o_ref.dtype)


def _dsa(proj3, tri):
    batch, seq, _ = proj3.shape
    topk = min(TOPK_MAX, seq // 4)
    class_len = seq // DSA_CLASSES
    tiles = class_len // DSA_TQ
    outs = []
    for c in range(DSA_CLASSES):
        width = (c + 1) * class_len
        base = c * tiles
        kern = functools.partial(_dsa_kernel, width=width, topk=topk, row0=c * class_len)
        outs.append(pl.pallas_call(
            kern,
            grid=(batch, tiles),
            in_specs=[
                pl.BlockSpec((None, DSA_TQ, ATTN_WIDTH),
                             lambda b, i, base=base: (b, base + i, COL_Q // ATTN_WIDTH)),
                pl.BlockSpec((None, DSA_TQ, IDX_HEADS * IDX_DIM),
                             lambda b, i, base=base: (b, base + i, COL_QI // (IDX_HEADS * IDX_DIM))),
                pl.BlockSpec((None, width, LANES), lambda b, i: (b, 0, COL_KV // LANES)),
                pl.BlockSpec((None, width, LANES), lambda b, i: (b, 0, COL_KIW // LANES)),
                pl.BlockSpec((None, DSA_TQ, LANES),
                             lambda b, i, base=base: (b, base + i, COL_KIW // LANES)),
                pl.BlockSpec((LANES, LANES), lambda b, i: (0, 0)),
            ],
            out_specs=pl.BlockSpec((None, DSA_TQ, ATTN_WIDTH), lambda b, i: (b, i, 0)),
            out_shape=jax.ShapeDtypeStruct((batch, class_len, ATTN_WIDTH), BF16),
            scratch_shapes=[
                pltpu.VMEM((width, LANES), BF16),
                pltpu.VMEM((DSA_TQ, width), F32),
                pltpu.VMEM((DSA_TQ, width), F32),
            ],
            compiler_params=_cparams(("arbitrary", "arbitrary")),
            name=f"dsa_attention_w{width}",
        )(proj3, proj3, proj3, proj3, proj3, tri))
    return jnp.concatenate(outs, axis=1)


MERGE_ROWS = 512


def _top2_gates(logits):
    lane = lax.broadcasted_iota(jnp.int32, logits.shape, 1)
    neg_inf = jnp.float32(-jnp.inf)
    lg = jnp.where(lane < N_EXPERTS, logits, neg_inf)
    v1 = jnp.max(lg, axis=-1, keepdims=True)
    i1 = jnp.min(jnp.where(lg == v1, lane, LANES), axis=-1, keepdims=True)
    rest = jnp.where(lane == i1, neg_inf, lg)
    v2 = jnp.max(rest, axis=-1, keepdims=True)
    i2 = jnp.min(jnp.where(rest == v2, lane, LANES), axis=-1, keepdims=True)
    e2 = jnp.exp(v2 - v1)
    den = 1.0 + e2
    return jnp.where(lane == i1, 1.0 / den, 0.0) + jnp.where(lane == i2, e2 / den, 0.0)


def _merge_kernel(*refs, with_router):
    if with_router:
        (at_ref, mab_ref, g2_ref, x_ref, woc_ref, wo_ref, ln_ref, rw_ref,
         x1_ref, h2_ref, gate_ref) = refs
    else:
        at_ref, mab_ref, g2_ref, x_ref, woc_ref, wo_ref, ln_ref, x1_ref, h2_ref = refs
    y_c = _dot(at_ref[...], woc_ref[...])
    mixed = mab_ref[...].astype(F32) + _sigmoid(g2_ref[...].astype(F32)) * y_c
    x1 = x_ref[...] + _dot(mixed.astype(BF16), wo_ref[...])
    x1_ref[...] = x1
    ms = jnp.mean(x1 * x1, axis=-1, keepdims=True)
    h2 = ((x1 * lax.rsqrt(ms + EPS)) * ln_ref[...]).astype(BF16)
    h2_ref[...] = h2
    if with_router:
        gate_ref[...] = _top2_gates(_dot(h2, rw_ref[...]))


def _merge(attn, mab, proj, x2d, w_out_c, w_o, ln2_g, router_w):
    m = x2d.shape[0]
    with_router = router_w is not None
    row = lambda i: (i, 0)
    const = lambda i: (0, 0)
    in_specs = [
        pl.BlockSpec((MERGE_ROWS, ATTN_WIDTH), row),
        pl.BlockSpec((MERGE_ROWS, D_MODEL), row),
        pl.BlockSpec((MERGE_ROWS, D_MODEL), lambda i: (i, COL_GATES // D_MODEL + 2)),
        pl.BlockSpec((MERGE_ROWS, D_MODEL), row),
        pl.BlockSpec((ATTN_WIDTH, D_MODEL), const),
        pl.BlockSpec((D_MODEL, D_MODEL), const),
        pl.BlockSpec((1, D_MODEL), const),
    ]
    out_specs = [pl.BlockSpec((MERGE_ROWS, D_MODEL), row), pl.BlockSpec((MERGE_ROWS, D_MODEL), row)]
    out_shape = [jax.ShapeDtypeStruct((m, D_MODEL), F32), jax.ShapeDtypeStruct((m, D_MODEL), BF16)]
    args = [attn, mab, proj, x2d, w_out_c, w_o, ln2_g]
    if with_router:
        in_specs.append(pl.BlockSpec((D_MODEL, LANES), const))
        out_specs.append(pl.BlockSpec((MERGE_ROWS, LANES), row))
        out_shape.append(jax.ShapeDtypeStruct((m, LANES), F32))
        args.append(router_w)
    return pl.pallas_call(
        functools.partial(_merge_kernel, with_router=with_router),
        grid=(m // MERGE_ROWS,),
        in_specs=in_specs,
        out_specs=out_specs,
        out_shape=out_shape,
        compiler_params=_cparams(("parallel",)),
        name="merge_out_router" if with_router else "merge_out",
    )(*args)


FFN_ROWS = 512
FFN_TILE = 1408


def _swiglu_part(h, w1, w3, w2):
    a = _dot(h, w1)
    b = _dot(h, w3)
    return _dot(((a * _sigmoid(a)) * b).astype(BF16), w2)


def _ffn_kernel(h_ref, x_ref, w1_ref, w3_ref, w2_ref, o_ref):
    @pl.when(pl.program_id(1) == 0)
    def _():
        o_ref[...] = x_ref[...]

    o_ref[...] += _swiglu_part(h_ref[...], w1_ref[...], w3_ref[...], w2_ref[...])


def _ffn(h2, x1, w1, w3, w2):
    m = x1.shape[0]
    d_ff = w1.shape[1]
    row = lambda i, f: (i, 0)
    return pl.pallas_call(
        _ffn_kernel,
        grid=(m // FFN_ROWS, d_ff // FFN_TILE),
        in_specs=[
            pl.BlockSpec((FFN_ROWS, D_MODEL), row),
            pl.BlockSpec((FFN_ROWS, D_MODEL), row),
            pl.BlockSpec((D_MODEL, FFN_TILE), lambda i, f: (0, f)),
            pl.BlockSpec((D_MODEL, FFN_TILE), lambda i, f: (0, f)),
            pl.BlockSpec((FFN_TILE, D_MODEL), lambda i, f: (f, 0)),
        ],
        out_specs=pl.BlockSpec((FFN_ROWS, D_MODEL), row),
        out_shape=jax.ShapeDtypeStruct((m, D_MODEL), F32),
        compiler_params=_cparams(("parallel", "arbitrary")),
        name="dense_ffn",
    )(h2, x1, w1, w3, w2)


MOE_TILE = 512
MOE_FF_TILE = 1792
MOE_BLK = 256
MOE_SRC_BLK = 512


def _count_le(sorted_ends, v):
    return jnp.sum((sorted_ends[None, :] <= v[:, None]).astype(jnp.int32), axis=1)


def _moe_route(gate):
    m = gate.shape[0]
    i32 = jnp.int32
    sel = gate[:, :N_EXPERTS] > 0.0
    seli = sel.astype(i32)
    csum = jnp.cumsum(seli, axis=0)
    rank = csum - seli
    cnt = csum[-1]
    gsz = (cnt + MOE_TILE - 1) // MOE_TILE * MOE_TILE
    gend = jnp.cumsum(gsz)
    goff = gend - gsz
    n_pos = 2 * m + N_EXPERTS * MOE_TILE
    pos = jnp.where(sel, goff[None, :] + rank, -1).astype(i32)

    n_ft = n_pos // MOE_TILE
    t0 = jnp.arange(n_ft, dtype=i32) * MOE_TILE
    tile_expert = jnp.minimum(_count_le(gend, t0), N_EXPERTS - 1)
    tile_active = (t0 < gend[-1]).astype(i32)

    n_db = n_pos // MOE_BLK
    n_sb = m // MOE_SRC_BLK
    d0 = jnp.arange(n_db, dtype=i32) * MOE_BLK
    e_d = jnp.minimum(_count_le(gend, d0), N_EXPERTS - 1)
    r0 = d0 - goff[e_d]
    cnt_d = cnt[e_d]
    has = (d0 < gend[-1]) & (r0 < cnt_d)
    r_last = jnp.minimum(r0 + MOE_BLK, cnt_d) - 1
    cb = csum[MOE_SRC_BLK - 1::MOE_SRC_BLK].T[e_d]
    b_lo = jnp.sum((cb <= r0[:, None]).astype(i32), axis=1)
    b_hi = jnp.sum((cb <= r_last[:, None]).astype(i32), axis=1)
    b_lo = jnp.where(has, b_lo, 0)
    n_it = jnp.where(has, b_hi - b_lo + 1, 1)
    it_end = jnp.cumsum(n_it)
    it_start = it_end - n_it
    n_items = n_db + N_EXPERTS * n_sb
    n = jnp.arange(n_items, dtype=i32)
    live = n < it_end[-1]
    d = jnp.minimum(_count_le(it_end, n), n_db - 1)
    item_tile = jnp.where(live, d, n_db).astype(i32)
    item_blk = jnp.where(live, b_lo[d] + n - it_start[d], 0).astype(i32)
    item_first = jnp.where(live, n == it_start[d], True).astype(i32)
    item_expert = jnp.where(live, e_d[d], 0).astype(i32)

    before = jnp.concatenate([jnp.zeros((1, N_EXPERTS), i32), csum], axis=0)[0:m:MOE_BLK]
    blk_a = ((goff[None, :] + before) // MOE_BLK).astype(i32).reshape(-1)
    return dict(pos=pos, pos_t=pos.T, tile_expert=tile_expert, tile_active=tile_active,
                item_tile=item_tile, item_blk=item_blk, item_first=item_first,
                item_expert=item_expert, blk_a=blk_a, n_pos=n_pos)


def _moe_gather_kernel(it_ref, ib_ref, if_ref, ie_ref, post_ref, h_ref, o_ref):
    n = pl.program_id(0)
    rel = post_ref[pl.ds(ie_ref[n], 1), :] - it_ref[n] * MOE_BLK
    row = lax.broadcasted_iota(jnp.int32, (MOE_BLK, MOE_SRC_BLK), 0)
    onehot = jnp.where(rel == row, 1.0, 0.0).astype(BF16)
    res = _dot(onehot, h_ref[...]).astype(o_ref.dtype)

    @pl.when(if_ref[n] == 1)
    def _():
        o_ref[...] = res

    @pl.when(if_ref[n] == 0)
    def _():
        o_ref[...] += res


def _moe_gather(route, h2):
    n_rows = route["n_pos"] + MOE_BLK
    n_items = route["item_tile"].shape[0]
    grid_spec = pltpu.PrefetchScalarGridSpec(
        num_scalar_prefetch=4,
        grid=(n_items,),
        in_specs=[
            pl.BlockSpec((N_EXPERTS, MOE_SRC_BLK), lambda n, it, ib, fi, ie: (0, ib[n])),
            pl.BlockSpec((MOE_SRC_BLK, D_MODEL), lambda n, it, ib, fi, ie: (ib[n], 0)),
        ],
        out_specs=pl.BlockSpec((MOE_BLK, D_MODEL), lambda n, it, ib, fi, ie: (it[n], 0)),
    )
    return pl.pallas_call(
        _moe_gather_kernel,
        grid_spec=grid_spec,
        out_shape=jax.ShapeDtypeStruct((n_rows, D_MODEL), BF16),
        compiler_params=_cparams(("arbitrary",)),
        name="moe_gather",
    )(route["item_tile"], route["item_blk"], route["item_first"], route["item_expert"],
      route["pos_t"], h2)


def _moe_ffn_kernel(te_ref, ta_ref, x_ref, w1_ref, w3_ref, w2_ref, o_ref, acc_ref):
    k = pl.program_id(0)
    f = pl.program_id(1)
    nf = pl.num_programs(1)
    active = ta_ref[k] == 1

    @pl.when(active)
    def _():
        part = _swiglu_part(x_ref[...], w1_ref[...], w3_ref[...], w2_ref[...])

        @pl.when(f == 0)
        def _():
            acc_ref[...] = part

        @pl.when(f > 0)
        def _():
            acc_ref[...] += part

        @pl.when(f == nf - 1)
        def _():
            o_ref[...] = acc_ref[...].astype(o_ref.dtype)

    @pl.when(jnp.logical_not(active) & (f == nf - 1))
    def _():
        o_ref[...] = jnp.zeros_like(o_ref)


def _moe_ffn(route, xg, w1, w3, w2):
    n_pos = route["n_pos"]
    d_ff = w1.shape[2]
    grid_spec = pltpu.PrefetchScalarGridSpec(
        num_scalar_prefetch=2,
        grid=(n_pos // MOE_TILE, d_ff // MOE_FF_TILE),
        in_specs=[
            pl.BlockSpec((MOE_TILE, D_MODEL), lambda k, f, te, ta: (k, 0)),
            pl.BlockSpec((None, D_MODEL, MOE_FF_TILE), lambda k, f, te, ta: (te[k], 0, f)),
            pl.BlockSpec((None, D_MODEL, MOE_FF_TILE), lambda k, f, te, ta: (te[k], 0, f)),
            pl.BlockSpec((None, MOE_FF_TILE, D_MODEL), lambda k, f, te, ta: (te[k], f, 0)),
        ],
        out_specs=pl.BlockSpec((MOE_TILE, D_MODEL), lambda k, f, te, ta: (k, 0)),
        scratch_shapes=[pltpu.VMEM((MOE_TILE, D_MODEL), F32)],
    )
    return pl.pallas_call(
        _moe_ffn_kernel,
        grid_spec=grid_spec,
        out_shape=jax.ShapeDtypeStruct((n_pos, D_MODEL), BF16),
        compiler_params=_cparams(("arbitrary", "arbitrary")),
        name="moe_expert_ffn",
    )(route["tile_expert"], route["tile_active"], xg, w1, w3, w2)


def _moe_combine_kernel(ba_ref, x_ref, gate_ref, pos_ref, *refs):
    y_refs, o_ref = refs[:-1], refs[-1]
    tb = pl.program_id(0)
    lane = lax.broadcasted_iota(jnp.int32, (MOE_BLK, MOE_BLK), 1)
    out = x_ref[...]
    for e in range(N_EXPERTS):
        base = ba_ref[tb * N_EXPERTS + e] * MOE_BLK
        rel = pos_ref[:, e:e + 1] - base
        picked = None
        for half in range(2):
            onehot = jnp.where(rel - half * MOE_BLK == lane, 1.0, 0.0).astype(BF16)
            part = _dot(onehot, y_refs[2 * e + half][...])
            picked = part if picked is None else picked + part
        out = out + gate_ref[:, e:e + 1] * picked
    o_ref[...] = out


def _moe_combine(route, x1, gate, y):
    m = x1.shape[0]
    last_blk = route["n_pos"] // MOE_BLK - 1
    row = lambda tb, ba: (tb, 0)
    y_specs = []
    for e in range(N_EXPERTS):
        for half in range(2):
            y_specs.append(pl.BlockSpec(
                (MOE_BLK, D_MODEL),
                lambda tb, ba, e=e, half=half: (
                    jnp.minimum(ba[tb * N_EXPERTS + e] + half, last_blk), 0)))
    grid_spec = pltpu.PrefetchScalarGridSpec(
        num_scalar_prefetch=1,
        grid=(m // MOE_BLK,),
        in_specs=[
            pl.BlockSpec((MOE_BLK, D_MODEL), row),
            pl.BlockSpec((MOE_BLK, LANES), row),
            pl.BlockSpec((MOE_BLK, N_EXPERTS), row),
        ] + y_specs,
        out_specs=pl.BlockSpec((MOE_BLK, D_MODEL), row),
    )
    return pl.pallas_call(
        _moe_combine_kernel,
        grid_spec=grid_spec,
        out_shape=jax.ShapeDtypeStruct((m, D_MODEL), F32),
        compiler_params=_cparams(("arbitrary",)),
        name="moe_combine",
    )(route["blk_a"], x1, gate, route["pos"], *([y] * (2 * N_EXPERTS)))


def _moe(h2, x1, gate, w1, w3, w2):
    route = _moe_route(gate)
    xg = _moe_gather(route, h2)
    y = _moe_ffn(route, xg, w1, w3, w2)
    return _moe_combine(route, x1, gate, y)


def _pad_in_proj(w_in):
    c_q_end = 2560
    c_k, c_v_end = 2560, 2688
    c_qi, c_qi_end = 2688, 2944
    c_ki, c_wi_end = 2944, 3012
    c_kiw_end = c_ki + LANES
    w = w_in.astype(BF16)
    return jnp.concatenate(
        [w[:, :c_q_end], w[:, c_qi:c_qi_end], w[:, c_k:c_v_end], w[:, c_ki:c_kiw_end],
         w[:, c_wi_end:]], axis=1)


def _s5_params(lam_re, lam_im, log_dt, b_re, b_im, c_re, c_im):
    lr = jnp.minimum(lam_re.astype(F32), -1e-4)
    li = lam_im.astype(F32)
    dt = jnp.exp(log_dt.astype(F32))[:, None]
    mag = jnp.exp(lr * dt)
    ang = li * dt
    ab_re = mag * jnp.cos(ang)
    ab_im = mag * jnp.sin(ang)
    nr = ab_re - 1.0
    ni = ab_im
    den = lr * lr + li * li
    coef_re = (nr * lr + ni * li) / den
    coef_im = (ni * lr - nr * li) / den
    bf_re = coef_re[:, :, None] * b_re - coef_im[:, :, None] * b_im
    bf_im = coef_re[:, :, None] * b_im + coef_im[:, :, None] * b_re
    gh = SSM_GROUPS // SSM_HALVES
    eye = jnp.eye(gh, dtype=F32)

    def blockdiag_b(w):
        w = w.reshape(SSM_HALVES, gh, SSM_STATE, SSM_GROUP)
        return jnp.einsum('jgph,gk->jghkp', w, eye).reshape(SSM_HALVES, SSM_HALF_CH, SSM_HALF_STATES)

    def blockdiag_c(w):
        w = w.reshape(SSM_HALVES, gh, SSM_GROUP, SSM_STATE)
        return jnp.einsum('jghp,gk->jgpkh', w, eye).reshape(SSM_HALVES, SSM_HALF_STATES, SSM_HALF_CH)

    bc = jnp.concatenate([blockdiag_b(bf_re), blockdiag_b(bf_im)], axis=2).astype(BF16)
    cc = jnp.concatenate([blockdiag_c(c_re.astype(F32)), -blockdiag_c(c_im.astype(F32))],
                         axis=1).astype(BF16)
    a_re8 = jnp.broadcast_to(ab_re.reshape(1, -1), (SUBLANES, SSM_GROUPS * SSM_STATE))
    a_im8 = jnp.broadcast_to(ab_im.reshape(1, -1), (SUBLANES, SSM_GROUPS * SSM_STATE))
    return bc, a_re8, a_im8, cc


def _rope_tables(seq):
    pos = jnp.arange(seq, dtype=F32)
    inv = ROPE_THETA ** (-jnp.arange(HEAD_DIM // 2, dtype=F32) / (HEAD_DIM // 2))
    ang = pos[:, None] * inv[None, :]
    cos, sin = jnp.cos(ang), jnp.sin(ang)
    cos2 = jnp.concatenate([cos, cos], axis=-1)
    sin2 = jnp.concatenate([-sin, sin], axis=-1)
    one, zero = jnp.ones_like(cos2), jnp.zeros_like(sin2)
    tab_a = jnp.concatenate([cos2, cos2, sin2, sin2], axis=-1)
    tab_b = jnp.concatenate([cos2, one, sin2, zero], axis=-1)
    return tab_a, tab_b


def kernel(x, ln1_g, w_in, conv_w, ssm_lam_re, ssm_lam_im, ssm_log_dt, ssm_b_re, ssm_b_im,
           ssm_c_re, ssm_c_im, ssm_d, ssm_w_glu, q_norm_g, k_norm_g, w_out_a, w_out_b, w_out_c,
           w_o, ln2_g, ffn_w1, ffn_w3, ffn_w2, router_w, moe_w1, moe_w3, moe_w2):
    batch, seq, d = x.shape
    assert batch == SUBLANES and d == D_MODEL
    depth = w_in.shape[0]
    m = batch * seq
    tab_a, tab_b = _rope_tables(seq)
    ii = lax.broadcasted_iota(jnp.int32, (LANES, LANES), 0)
    jj = lax.broadcasted_iota(jnp.int32, (LANES, LANES), 1)
    tri = (ii < jj).astype(BF16)
    gsum = ((ii // HEAD_DIM) == (jj // HEAD_DIM)).astype(BF16)
    r_tm = lax.broadcasted_iota(jnp.int32, (MIX_ROWS, MIX_ROWS), 0)
    r_bm = lax.broadcasted_iota(jnp.int32, (MIX_ROWS, MIX_ROWS), 1)
    perm = ((r_tm % SUBLANES) * MIX_STEPS + r_tm // SUBLANES == r_bm).astype(BF16)
    perm_t = perm.T

    xt = x.reshape(m, d)
    for layer in range(depth):
        q_gain2 = jnp.tile(q_norm_g[layer], 2)[None, :]
        k_gain2 = jnp.tile(k_norm_g[layer], 2)[None, :]
        proj = _inproj(xt, ln1_g[layer][None, :], _pad_in_proj(w_in[layer]), tab_a, tab_b,
                       q_gain2, k_gain2, gsum, seq)
        proj3 = proj.reshape(batch, seq, PROJ_COLS)
        bc, a_re8, a_im8, cc = _s5_params(
            ssm_lam_re[layer], ssm_lam_im[layer], ssm_log_dt[layer], ssm_b_re[layer],
            ssm_b_im[layer], ssm_c_re[layer], ssm_c_im[layer])
        mab = _mixer(proj3, conv_w[layer], perm, perm_t, bc, a_re8, a_im8, cc,
                     ssm_d[layer][None, :], ssm_w_glu[layer].astype(BF16),
                     w_out_a[layer].astype(BF16), w_out_b[layer].astype(BF16))
        mab = mab.reshape(m, D_MODEL)
        attn = _dsa(proj3, tri).reshape(m, ATTN_WIDTH)
        j = layer // 2
        if layer % 2 == 0:
            x1, h2 = _merge(attn, mab, proj, xt, w_out_c[layer].astype(BF16),
                            w_o[layer].astype(BF16), ln2_g[layer][None, :], None)
            xt = _ffn(h2, x1, ffn_w1[j].astype(BF16), ffn_w3[j].astype(BF16),
                      ffn_w2[j].astype(BF16))
        else:
            rw = jnp.pad(router_w[j], ((0, 0), (0, LANES - N_EXPERTS))).astype(BF16)
            x1, h2, gate = _merge(attn, mab, proj, xt, w_out_c[layer].astype(BF16),
                                  w_o[layer].astype(BF16), ln2_g[layer][None, :], rw)
            xt = _moe(h2, x1, gate, moe_w1[j].astype(BF16), moe_w3[j].astype(BF16),
                      moe_w2[j].astype(BF16))
    return xt.reshape(batch, seq, d)
```

```python
import functools

import jax
import jax.numpy as jnp
from jax import lax
from jax.experimental import pallas as pl
from jax.experimental.pallas import tpu as pltpu

F32 = jnp.float32
BF16 = jnp.bfloat16

D_MODEL = 1024
CONV_WIDTH = 512
CONV_K = 3
SSM_WIDTH = 512
SSM_GROUP = 16
SSM_GROUPS = 32
SSM_STATE = 64
ATTN_HEADS = 8
HEAD_DIM = 64
ATTN_WIDTH = 512
IDX_HEADS = 4
IDX_DIM = 64
TOPK_MAX = 256
ROPE_THETA = 10000.0
N_EXPERTS = 8
EPS = 1e-6
LOG2_E = 1.4426950408889634

SUBLANES = 8
LANES = 128
VMEM_LIMIT_BYTES = 56 * 1024 * 1024

COL_CONV = 0
COL_SSM = 1536
COL_Q = 2048
COL_QI = 2560
COL_KV = 2816
COL_KIW = 2944
COL_GATES = 3072
PROJ_COLS = 6144

SSM_HALVES = 2
SSM_HALF_STATES = SSM_GROUPS // SSM_HALVES * SSM_STATE
SSM_HALF_CH = SSM_WIDTH // SSM_HALVES
SSM_COLS = 2 * SSM_GROUPS * SSM_STATE
SCAN_COLS = 512

BISECT_ITERS = 28
BISECT_UNROLL = 4


def _cparams(sem):
    return pltpu.CompilerParams(dimension_semantics=sem, vmem_limit_bytes=VMEM_LIMIT_BYTES)


def _sigmoid(x):
    return 1.0 / (1.0 + jnp.exp(-x))


def _dot(a, b):
    return jnp.dot(a, b, preferred_element_type=F32)


def _dot_nt(a, b):
    return lax.dot_general(a, b, (((1,), (1,)), ((), ())), preferred_element_type=F32)


INPROJ_ROWS = 512
INPROJ_NCHUNK = 512


def _rope_block(x, cos, sin):
    lane = lax.broadcasted_iota(jnp.int32, x.shape, 1)
    first_half = (lane % HEAD_DIM) < (HEAD_DIM // 2)
    swapped = jnp.where(first_half, pltpu.roll(x, LANES - HEAD_DIM // 2, 1),
                        pltpu.roll(x, HEAD_DIM // 2, 1))
    return x * cos + swapped * sin


def _head_inv_rms(x, gsum):
    sq = x * x
    hi = sq.astype(BF16)
    lo = (sq - hi.astype(F32)).astype(BF16)
    ss = _dot(hi, gsum) + _dot(lo, gsum)
    return lax.rsqrt(ss * (1.0 / HEAD_DIM) + EPS)


def _inproj_kernel(x_ref, g_ref, w_ref, taba_ref, tabb_ref, qg_ref, kg_ref, gsum_ref, o_ref):
    x = x_ref[...]
    ms = jnp.mean(x * x, axis=-1, keepdims=True)
    h = ((x * lax.rsqrt(ms + EPS)) * g_ref[...]).astype(BF16)
    gsum = gsum_ref[...]
    cos_a, sin_a = taba_ref[:, 0:LANES], taba_ref[:, LANES:2 * LANES]
    cos_b, sin_b = tabb_ref[:, 0:LANES], tabb_ref[:, LANES:2 * LANES]
    for n in range(PROJ_COLS // INPROJ_NCHUNK):
        c0 = n * INPROJ_NCHUNK
        res = _dot(h, w_ref[:, c0:c0 + INPROJ_NCHUNK])
        if c0 == COL_Q:
            for blk in range(INPROJ_NCHUNK // LANES):
                xb = res[:, blk * LANES:(blk + 1) * LANES]
                xb = (xb * _head_inv_rms(xb, gsum)) * qg_ref[...]
                xb = _rope_block(xb, cos_a, sin_a) * (HEAD_DIM ** -0.5 * LOG2_E)
                o_ref[:, c0 + blk * LANES:c0 + (blk + 1) * LANES] = xb.astype(o_ref.dtype)
        elif c0 == COL_QI:
            for blk in range(INPROJ_NCHUNK // LANES):
                xb = res[:, blk * LANES:(blk + 1) * LANES]
                col = c0 + blk * LANES
                if col < COL_KV:
                    xb = _rope_block(xb, cos_a, sin_a)
                elif col == COL_KV:
                    lane = lax.broadcasted_iota(jnp.int32, xb.shape, 1)
                    xn = (xb * _head_inv_rms(xb, gsum)) * kg_ref[...]
                    xb = _rope_block(jnp.where(lane < HEAD_DIM, xn, xb), cos_b, sin_b)
                else:
                    xb = _rope_block(xb, cos_b, sin_b)
                o_ref[:, col:col + LANES] = xb.astype(o_ref.dtype)
        else:
            o_ref[:, c0:c0 + INPROJ_NCHUNK] = res.astype(o_ref.dtype)


def _inproj(x2d, ln_g, w_in_p, tab_a, tab_b, q_gain2, k_gain2, gsum, seq):
    m = x2d.shape[0]
    tiles_per_seq = seq // INPROJ_ROWS
    const = lambda i: (0, 0)
    return pl.pallas_call(
        _inproj_kernel,
        grid=(m // INPROJ_ROWS,),
        in_specs=[
            pl.BlockSpec((INPROJ_ROWS, D_MODEL), lambda i: (i, 0)),
            pl.BlockSpec((1, D_MODEL), const),
            pl.BlockSpec((D_MODEL, PROJ_COLS), const),
            pl.BlockSpec((INPROJ_ROWS, 2 * LANES), lambda i: (i % tiles_per_seq, 0)),
            pl.BlockSpec((INPROJ_ROWS, 2 * LANES), lambda i: (i % tiles_per_seq, 0)),
            pl.BlockSpec((1, LANES), const),
            pl.BlockSpec((1, LANES), const),
            pl.BlockSpec((LANES, LANES), const),
        ],
        out_specs=pl.BlockSpec((INPROJ_ROWS, PROJ_COLS), lambda i: (i, 0)),
        out_shape=jax.ShapeDtypeStruct((m, PROJ_COLS), BF16),
        compiler_params=_cparams(("parallel",)),
        name="inproj",
    )(x2d, ln_g, w_in_p, tab_a, tab_b, q_gain2, k_gain2, gsum)


MIX_STEPS = 64
MIX_ROWS = MIX_STEPS * SUBLANES


def _mixer_kernel(pa_ref, su_ref, g0_ref, g1_ref, cw_ref, perm_ref, permt_ref, bc_ref, are_ref,
                  aim_ref, cc_ref, d_ref, wglu_ref, woa_ref, wob_ref, o_ref,
                  xs_ref, hst_ref, vext_ref):
    i = pl.program_id(0)
    rows = MIX_ROWS
    tc = MIX_STEPS

    @pl.when(i == 0)
    def _():
        hst_ref[...] = jnp.zeros_like(hst_ref)
        vext_ref[:, 0:SUBLANES, :] = jnp.zeros((SUBLANES, SUBLANES, CONV_WIDTH), F32)

    u = pa_ref[:, :, 0:CONV_WIDTH].astype(F32)
    bg = pa_ref[:, :, CONV_WIDTH:2 * CONV_WIDTH].astype(F32)
    cg = pa_ref[:, :, 2 * CONV_WIDTH:3 * CONV_WIDTH].astype(F32)
    v = cg * u
    vext_ref[:, SUBLANES:SUBLANES + tc, :] = v
    y = (cw_ref[0:1, :] * vext_ref[:, SUBLANES - 2:SUBLANES - 2 + tc, :]
         + cw_ref[1:2, :] * vext_ref[:, SUBLANES - 1:SUBLANES - 1 + tc, :]
         + cw_ref[2:3, :] * v)
    vext_ref[:, 0:SUBLANES, :] = v[:, tc - SUBLANES:tc, :]
    y_a = _dot((bg * y).reshape(rows, CONV_WIDTH).astype(BF16), woa_ref[...])

    u_tm = _dot(perm_ref[...], su_ref[...].reshape(rows, SSM_WIDTH))
    u_tm_b = u_tm.astype(BF16)
    for j in range(SSM_HALVES):
        xs_ref[:, j * 2 * SSM_HALF_STATES:(j + 1) * 2 * SSM_HALF_STATES] = _dot(
            u_tm_b[:, j * SSM_HALF_CH:(j + 1) * SSM_HALF_CH], bc_ref[j])

    for j in range(SSM_HALVES):
        for q in range(SSM_HALF_STATES // SCAN_COLS):
            cr = j * 2 * SSM_HALF_STATES + q * SCAN_COLS
            ci = cr + SSM_HALF_STATES
            ca = j * SSM_HALF_STATES + q * SCAN_COLS
            ar = are_ref[:, ca:ca + SCAN_COLS]
            ai = aim_ref[:, ca:ca + SCAN_COLS]

            def step(t, carry, cr=cr, ci=ci, ar=ar, ai=ai):
                hr, hi = carry
                r0 = pl.multiple_of(t * SUBLANES, SUBLANES)
                xr = xs_ref[pl.ds(r0, SUBLANES), cr:cr + SCAN_COLS]
                xi = xs_ref[pl.ds(r0, SUBLANES), ci:ci + SCAN_COLS]
                nr = ar * hr - ai * hi + xr
                ni = ar * hi + ai * hr + xi
                xs_ref[pl.ds(r0, SUBLANES), cr:cr + SCAN_COLS] = nr
                xs_ref[pl.ds(r0, SUBLANES), ci:ci + SCAN_COLS] = ni
                return nr, ni

            hr, hi = lax.fori_loop(
                0, MIX_STEPS, step,
                (hst_ref[:, cr:cr + SCAN_COLS], hst_ref[:, ci:ci + SCAN_COLS]), unroll=4)
            hst_ref[:, cr:cr + SCAN_COLS] = hr
            hst_ref[:, ci:ci + SCAN_COLS] = hi

    ys = []
    for j in range(SSM_HALVES):
        hj = xs_ref[:, j * 2 * SSM_HALF_STATES:(j + 1) * 2 * SSM_HALF_STATES].astype(BF16)
        ys.append(_dot(hj, cc_ref[j]))
    ysum = jnp.concatenate(ys, axis=-1) + d_ref[...] * u_tm
    yg = jax.nn.gelu(ysum, approximate=True)
    yg = (yg * _sigmoid(_dot(yg.astype(BF16), wglu_ref[...]))).astype(BF16)
    yg_bm = _dot(permt_ref[...], yg).astype(BF16)
    y_b = _dot(yg_bm, wob_ref[...])

    g0 = g0_ref[...].reshape(rows, D_MODEL).astype(F32)
    g1 = g1_ref[...].reshape(rows, D_MODEL).astype(F32)
    out = _sigmoid(g0) * y_a + _sigmoid(g1) * y_b
    o_ref[...] = out.reshape(SUBLANES, tc, D_MODEL).astype(o_ref.dtype)


def _mixer(proj3, conv_w, perm, perm_t, bc, a_re8, a_im8, cc, d_skip, w_glu, w_out_a, w_out_b):
    batch, seq, _ = proj3.shape
    const2 = lambda i: (0, 0)
    const3 = lambda i: (0, 0, 0)
    return pl.pallas_call(
        _mixer_kernel,
        grid=(seq // MIX_STEPS,),
        in_specs=[
            pl.BlockSpec((batch, MIX_STEPS, 3 * CONV_WIDTH),
                         lambda i: (0, i, COL_CONV // (3 * CONV_WIDTH))),
            pl.BlockSpec((batch, MIX_STEPS, SSM_WIDTH), lambda i: (0, i, COL_SSM // SSM_WIDTH)),
            pl.BlockSpec((batch, MIX_STEPS, D_MODEL), lambda i: (0, i, COL_GATES // D_MODEL)),
            pl.BlockSpec((batch, MIX_STEPS, D_MODEL), lambda i: (0, i, COL_GATES // D_MODEL + 1)),
            pl.BlockSpec((CONV_K, CONV_WIDTH), const2),
            pl.BlockSpec((MIX_ROWS, MIX_ROWS), const2),
            pl.BlockSpec((MIX_ROWS, MIX_ROWS), const2),
            pl.BlockSpec((SSM_HALVES, SSM_HALF_CH, 2 * SSM_HALF_STATES), const3),
            pl.BlockSpec((SUBLANES, SSM_GROUPS * SSM_STATE), const2),
            pl.BlockSpec((SUBLANES, SSM_GROUPS * SSM_STATE), const2),
            pl.BlockSpec((SSM_HALVES, 2 * SSM_HALF_STATES, SSM_HALF_CH), const3),
            pl.BlockSpec((1, SSM_WIDTH), const2),
            pl.BlockSpec((SSM_WIDTH, SSM_WIDTH), const2),
            pl.BlockSpec((CONV_WIDTH, D_MODEL), const2),
            pl.BlockSpec((SSM_WIDTH, D_MODEL), const2),
        ],
        out_specs=pl.BlockSpec((batch, MIX_STEPS, D_MODEL), lambda i: (0, i, 0)),
        out_shape=jax.ShapeDtypeStruct((batch, seq, D_MODEL), BF16),
        scratch_shapes=[
            pltpu.VMEM((MIX_ROWS, SSM_COLS), F32),
            pltpu.VMEM((SUBLANES, SSM_COLS), F32),
            pltpu.VMEM((SUBLANES, MIX_STEPS + SUBLANES, CONV_WIDTH), F32),
        ],
        compiler_params=_cparams(("arbitrary",)),
        name="conv_s5_mixer",
    )(proj3, proj3, proj3, proj3, conv_w, perm, perm_t, bc, a_re8, a_im8, cc, d_skip, w_glu,
      w_out_a, w_out_b)


DSA_TQ = 512
DSA_CLASSES = 4
DSA_CHAINS = 2
DSA_HEAD_GROUP = 4


def _dsa_kernel(q_ref, qi_ref, kv_ref, kiw_ref, kiwq_ref, tri_ref, o_ref,
                vext_ref, score_ref, bias_ref, *, tq, width, topk, row0):
    i = pl.program_id(1)
    rows_c = tq // DSA_CHAINS
    neg_inf = jnp.float32(-jnp.inf)
    pos_inf = jnp.float32(jnp.inf)

    @pl.when(i == 0)
    def _():
        lane = lax.broadcasted_iota(jnp.int32, (width, LANES), 1)
        shifted = pltpu.roll(kv_ref[...].astype(F32), HEAD_DIM, 1)
        vext_ref[...] = jnp.where(lane < HEAD_DIM, shifted, 1.0).astype(BF16)

    t0 = row0 + i * tq

    wq = kiwq_ref[:, IDX_DIM:IDX_DIM + IDX_HEADS].astype(F32) * (IDX_HEADS ** -0.5)
    kirot = kiw_ref[:, 0:IDX_DIM]
    score = None
    for h in range(IDX_HEADS):
        rel = jnp.maximum(_dot_nt(qi_ref[:, h * IDX_DIM:(h + 1) * IDX_DIM], kirot), 0.0)
        rel = rel * wq[:, h:h + 1]
        score = rel if score is None else score + rel
    col = lax.broadcasted_iota(jnp.int32, (tq, width), 1)
    row = t0 + lax.broadcasted_iota(jnp.int32, (tq, width), 0)
    score_ref[...] = jnp.where(col <= row, score, neg_inf)

    def chain_rows(c):
        return slice(c * rows_c, (c + 1) * rows_c)

    kks, los, his, fins = [], [], [], []
    for c in range(DSA_CHAINS):
        sc = score_ref[chain_rows(c), :]
        t_row = t0 + c * rows_c + lax.broadcasted_iota(jnp.int32, (rows_c, 1), 0)
        kk_c = jnp.minimum(t_row + 1, topk).astype(F32)
        hi_c = jnp.max(sc, axis=-1, keepdims=True)
        lo_c = jnp.min(jnp.where(sc == neg_inf, pos_inf, sc), axis=-1, keepdims=True)
        n_ge0 = jnp.sum(jnp.where(sc >= 0.0, 1.0, 0.0), axis=-1, keepdims=True)
        n_gt0 = jnp.sum(jnp.where(sc > 0.0, 1.0, 0.0), axis=-1, keepdims=True)
        above0 = n_gt0 >= kk_c
        reach0 = n_ge0 >= kk_c
        take_all = t_row < topk
        kks.append(kk_c)
        los.append(jnp.where(take_all, lo_c, jnp.where(reach0, 0.0, lo_c)))
        his.append(jnp.where(take_all, lo_c, jnp.where(above0, hi_c, 0.0)))
        fins.append(jnp.where(take_all | (reach0 & jnp.logical_not(above0)), 1.0, 0.0))

    def bisect_round(carry):
        it, los_c, his_c, fins_c, _ = carry
        open_rows = jnp.float32(0.0)
        for _ in range(BISECT_UNROLL):
            new_lo, new_hi, new_fin = [], [], []
            open_rows = jnp.float32(0.0)
            for c in range(DSA_CHAINS):
                sc = score_ref[chain_rows(c), :]
                mid = 0.5 * los_c[c] + 0.5 * his_c[c]
                cnt = jnp.sum(jnp.where(sc >= mid, 1.0, 0.0), axis=-1, keepdims=True)
                live = fins_c[c] == 0.0
                ge = cnt >= kks[c]
                hit = live & (cnt == kks[c])
                new_lo.append(jnp.where(live & ge, mid, los_c[c]))
                new_hi.append(jnp.where(live & (hit | jnp.logical_not(ge)), mid, his_c[c]))
                fin = jnp.where(hit, 1.0, fins_c[c])
                new_fin.append(fin)
                open_rows = jnp.maximum(open_rows, jnp.max(1.0 - fin))
            los_c, his_c, fins_c = tuple(new_lo), tuple(new_hi), tuple(new_fin)
        return it + BISECT_UNROLL, los_c, his_c, fins_c, open_rows

    def bisect_cond(carry):
        return (carry[0] < BISECT_ITERS) & (carry[4] > 0.0)

    _, los, _, _, _ = lax.while_loop(
        bisect_cond, bisect_round,
        (jnp.int32(0), tuple(los), tuple(his), tuple(fins), jnp.float32(1.0)))
    lo = jnp.concatenate(los, axis=0)
    kk = jnp.concatenate(kks, axis=0)

    def count_gt(thr):
        return jnp.sum(jnp.where(score_ref[...] > thr, 1.0, 0.0), axis=-1, keepdims=True)

    def snap_from(lo_v, strict):
        sc = score_ref[...]
        m = (sc > lo_v) if strict else (sc >= lo_v)
        return jnp.min(jnp.where(m, sc, pos_inf), axis=-1, keepdims=True)

    thr0 = snap_from(lo, False)
    above0 = count_gt(thr0)

    def refine_cond(carry):
        _, above = carry
        return jnp.max(jnp.where(above >= kk, 1.0, 0.0)) > 0.0

    def refine_body(carry):
        thr, above = carry
        thr = jnp.where(above >= kk, snap_from(thr, True), thr)
        return thr, count_gt(thr)

    thr, above = lax.while_loop(refine_cond, refine_body, (thr0, above0))

    need = kk - above
    ones_blk = jnp.ones((LANES, LANES), BF16)
    offs = jnp.zeros((tq, LANES), F32)
    for c in range(width // LANES):
        sl = slice(c * LANES, (c + 1) * LANES)
        sc = score_ref[:, sl]
        tie = sc == thr
        tie_b = jnp.where(tie, 1.0, 0.0).astype(BF16)
        before = _dot(tie_b, tri_ref[...]) + offs
        sel = (sc > thr) | (tie & (before < need))
        bias_ref[:, sl] = jnp.where(sel, 0.0, neg_inf)
        offs = offs + _dot(tie_b, ones_blk)

    krot = kv_ref[:, 0:HEAD_DIM]
    vext = vext_ref[...]
    for h0 in range(0, ATTN_HEADS, DSA_HEAD_GROUP):
        heads = range(h0, h0 + DSA_HEAD_GROUP)
        lgs = [_dot_nt(q_ref[:, h * HEAD_DIM:(h + 1) * HEAD_DIM], krot) + bias_ref[...]
               for h in heads]
        mxs = [jnp.max(lg, axis=-1, keepdims=True) for lg in lgs]
        ps = [jnp.exp2(lg - mx).astype(BF16) for lg, mx in zip(lgs, mxs)]
        ovs = [_dot(p, vext) for p in ps]
        for h, ov in zip(heads, ovs):
            oh = ov[:, 0:HEAD_DIM] / ov[:, HEAD_DIM:HEAD_DIM + 1]
            o_ref[:, h * HEAD_DIM:(h + 1) * HEAD_DIM] = oh.astype(o_ref.dtype)


def _dsa(proj3, tri):
    batch, seq, _ = proj3.shape
    topk = min(TOPK_MAX, seq // 4)
    class_len = seq // DSA_CLASSES
    tq = min(DSA_TQ, class_len)
    tiles = class_len // tq
    outs = []
    for c in range(DSA_CLASSES):
        width = (c + 1) * class_len
        base = c * tiles
        kern = functools.partial(_dsa_kernel, tq=tq, width=width, topk=topk,
                                 row0=c * class_len)
        outs.append(pl.pallas_call(
            kern,
            grid=(batch, tiles),
            in_specs=[
                pl.BlockSpec((None, tq, ATTN_WIDTH),
                             lambda b, i, base=base: (b, base + i, COL_Q // ATTN_WIDTH)),
                pl.BlockSpec((None, tq, IDX_HEADS * IDX_DIM),
                             lambda b, i, base=base: (b, base + i, COL_QI // (IDX_HEADS * IDX_DIM))),
                pl.BlockSpec((None, width, LANES), lambda b, i: (b, 0, COL_KV // LANES)),
                pl.BlockSpec((None, width, LANES), lambda b, i: (b, 0, COL_KIW // LANES)),
                pl.BlockSpec((None, tq, LANES),
                             lambda b, i, base=base: (b, base + i, COL_KIW // LANES)),
                pl.BlockSpec((LANES, LANES), lambda b, i: (0, 0)),
            ],
            out_specs=pl.BlockSpec((None, tq, ATTN_WIDTH), lambda b, i: (b, i, 0)),
            out_shape=jax.ShapeDtypeStruct((batch, class_len, ATTN_WIDTH), BF16),
            scratch_shapes=[
                pltpu.VMEM((width, LANES), BF16),
                pltpu.VMEM((tq, width), F32),
                pltpu.VMEM((tq, width), F32),
            ],
            compiler_params=_cparams(("arbitrary", "arbitrary")),
            name=f"dsa_attention_w{width}",
        )(proj3, proj3, proj3, proj3, proj3, tri))
    return jnp.concatenate(outs, axis=1)


MERGE_ROWS = 512


def _top2_gates(logits):
    lane = lax.broadcasted_iota(jnp.int32, logits.shape, 1)
    neg_inf = jnp.float32(-jnp.inf)
    lg = jnp.where(lane < N_EXPERTS, logits, neg_inf)
    v1 = jnp.max(lg, axis=-1, keepdims=True)
    i1 = jnp.min(jnp.where(lg == v1, lane, LANES), axis=-1, keepdims=True)
    rest = jnp.where(lane == i1, neg_inf, lg)
    v2 = jnp.max(rest, axis=-1, keepdims=True)
    i2 = jnp.min(jnp.where(rest == v2, lane, LANES), axis=-1, keepdims=True)
    e2 = jnp.exp(v2 - v1)
    den = 1.0 + e2
    return jnp.where(lane == i1, 1.0 / den, 0.0) + jnp.where(lane == i2, e2 / den, 0.0)


def _merge_kernel(*refs, with_router):
    if with_router:
        (at_ref, mab_ref, g2_ref, x_ref, woc_ref, wo_ref, ln_ref, rw_ref,
         x1_ref, h2_ref, gate_ref) = refs
    else:
        at_ref, mab_ref, g2_ref, x_ref, woc_ref, wo_ref, ln_ref, x1_ref, h2_ref = refs
    y_c = _dot(at_ref[...], woc_ref[...])
    mixed = mab_ref[...].astype(F32) + _sigmoid(g2_ref[...].astype(F32)) * y_c
    x1 = x_ref[...] + _dot(mixed.astype(BF16), wo_ref[...])
    x1_ref[...] = x1
    ms = jnp.mean(x1 * x1, axis=-1, keepdims=True)
    h2 = ((x1 * lax.rsqrt(ms + EPS)) * ln_ref[...]).astype(BF16)
    h2_ref[...] = h2
    if with_router:
        gate_ref[...] = _top2_gates(_dot(h2, rw_ref[...]))


def _merge(attn, mab, proj, x2d, w_out_c, w_o, ln2_g, router_w):
    m = x2d.shape[0]
    with_router = router_w is not None
    row = lambda i: (i, 0)
    const = lambda i: (0, 0)
    in_specs = [
        pl.BlockSpec((MERGE_ROWS, ATTN_WIDTH), row),
        pl.BlockSpec((MERGE_ROWS, D_MODEL), row),
        pl.BlockSpec((MERGE_ROWS, D_MODEL), lambda i: (i, COL_GATES // D_MODEL + 2)),
        pl.BlockSpec((MERGE_ROWS, D_MODEL), row),
        pl.BlockSpec((ATTN_WIDTH, D_MODEL), const),
        pl.BlockSpec((D_MODEL, D_MODEL), const),
        pl.BlockSpec((1, D_MODEL), const),
    ]
    out_specs = [pl.BlockSpec((MERGE_ROWS, D_MODEL), row), pl.BlockSpec((MERGE_ROWS, D_MODEL), row)]
    out_shape = [jax.ShapeDtypeStruct((m, D_MODEL), F32), jax.ShapeDtypeStruct((m, D_MODEL), BF16)]
    args = [attn, mab, proj, x2d, w_out_c, w_o, ln2_g]
    if with_router:
        in_specs.append(pl.BlockSpec((D_MODEL, LANES), const))
        out_specs.append(pl.BlockSpec((MERGE_ROWS, LANES), row))
        out_shape.append(jax.ShapeDtypeStruct((m, LANES), F32))
        args.append(router_w)
    return pl.pallas_call(
        functools.partial(_merge_kernel, with_router=with_router),
        grid=(m // MERGE_ROWS,),
        in_specs=in_specs,
        out_specs=out_specs,
        out_shape=out_shape,
        compiler_params=_cparams(("parallel",)),
        name="merge_out_router" if with_router else "merge_out",
    )(*args)


FFN_ROWS = 512
FFN_TILE = 1408


def _swiglu_part(h, w1, w3, w2):
    a = _dot(h, w1)
    b = _dot(h, w3)
    return _dot(((a * _sigmoid(a)) * b).astype(BF16), w2)


def _ffn_kernel(h_ref, x_ref, w1_ref, w3_ref, w2_ref, o_ref):
    @pl.when(pl.program_id(1) == 0)
    def _():
        o_ref[...] = x_ref[...]

    o_ref[...] += _swiglu_part(h_ref[...], w1_ref[...], w3_ref[...], w2_ref[...])


def _ffn(h2, x1, w1, w3, w2):
    m = x1.shape[0]
    d_ff = w1.shape[1]
    row = lambda i, f: (i, 0)
    return pl.pallas_call(
        _ffn_kernel,
        grid=(m // FFN_ROWS, d_ff // FFN_TILE),
        in_specs=[
            pl.BlockSpec((FFN_ROWS, D_MODEL), row),
            pl.BlockSpec((FFN_ROWS, D_MODEL), row),
            pl.BlockSpec((D_MODEL, FFN_TILE), lambda i, f: (0, f)),
            pl.BlockSpec((D_MODEL, FFN_TILE), lambda i, f: (0, f)),
            pl.BlockSpec((FFN_TILE, D_MODEL), lambda i, f: (f, 0)),
        ],
        out_specs=pl.BlockSpec((FFN_ROWS, D_MODEL), row),
        out_shape=jax.ShapeDtypeStruct((m, D_MODEL), F32),
        compiler_params=_cparams(("parallel", "arbitrary")),
        name="dense_ffn",
    )(h2, x1, w1, w3, w2)


MOE_TILE = 512
MOE_FF_TILE = 1792
MOE_BLK = 256
MOE_SRC_BLK = 1024


def _count_le(sorted_ends, v):
    return jnp.sum((sorted_ends[None, :] <= v[:, None]).astype(jnp.int32), axis=1)


def _moe_route(gate):
    m = gate.shape[0]
    i32 = jnp.int32
    sel = gate[:, :N_EXPERTS] > 0.0
    seli = sel.astype(i32)
    csum = jnp.cumsum(seli, axis=0)
    rank = csum - seli
    cnt = csum[-1]
    gsz = (cnt + MOE_TILE - 1) // MOE_TILE * MOE_TILE
    gend = jnp.cumsum(gsz)
    goff = gend - gsz
    n_pos = 2 * m + N_EXPERTS * MOE_TILE
    pos = jnp.where(sel, goff[None, :] + rank, -1).astype(i32)

    n_ft = n_pos // MOE_TILE
    t0 = jnp.arange(n_ft, dtype=i32) * MOE_TILE
    tile_expert = jnp.minimum(_count_le(gend, t0), N_EXPERTS - 1)
    tile_active = (t0 < gend[-1]).astype(i32)

    n_db = n_pos // MOE_BLK
    n_sb = m // MOE_SRC_BLK
    d0 = jnp.arange(n_db, dtype=i32) * MOE_BLK
    e_d = jnp.minimum(_count_le(gend, d0), N_EXPERTS - 1)
    r0 = d0 - goff[e_d]
    cnt_d = cnt[e_d]
    has = (d0 < gend[-1]) & (r0 < cnt_d)
    r_last = jnp.minimum(r0 + MOE_BLK, cnt_d) - 1
    cb = csum[MOE_SRC_BLK - 1::MOE_SRC_BLK].T[e_d]
    b_lo = jnp.sum((cb <= r0[:, None]).astype(i32), axis=1)
    b_hi = jnp.sum((cb <= r_last[:, None]).astype(i32), axis=1)
    b_lo = jnp.where(has, b_lo, 0)
    n_it = jnp.where(has, b_hi - b_lo + 1, 1)
    it_end = jnp.cumsum(n_it)
    it_start = it_end - n_it
    n_items = n_db + N_EXPERTS * n_sb
    n = jnp.arange(n_items, dtype=i32)
    live = n < it_end[-1]
    d = jnp.minimum(_count_le(it_end, n), n_db - 1)
    item_tile = jnp.where(live, d, n_db).astype(i32)
    item_blk = jnp.where(live, b_lo[d] + n - it_start[d], 0).astype(i32)
    item_first = jnp.where(live, n == it_start[d], True).astype(i32)
    item_expert = jnp.where(live, e_d[d], 0).astype(i32)

    before = jnp.concatenate([jnp.zeros((1, N_EXPERTS), i32), csum], axis=0)[0:m:MOE_BLK]
    blk_a = ((goff[None, :] + before) // MOE_BLK).astype(i32).reshape(-1)
    return dict(pos=pos, pos_t=pos.T, tile_expert=tile_expert, tile_active=tile_active,
                item_tile=item_tile, item_blk=item_blk, item_first=item_first,
                item_expert=item_expert, blk_a=blk_a, n_pos=n_pos)


def _moe_gather_kernel(it_ref, ib_ref, if_ref, ie_ref, post_ref, h_ref, o_ref):
    n = pl.program_id(0)
    rel = post_ref[pl.ds(ie_ref[n], 1), :] - it_ref[n] * MOE_BLK
    row = lax.broadcasted_iota(jnp.int32, (MOE_BLK, MOE_SRC_BLK), 0)
    onehot = jnp.where(rel == row, 1.0, 0.0).astype(BF16)
    res = _dot(onehot, h_ref[...]).astype(o_ref.dtype)

    @pl.when(if_ref[n] == 1)
    def _():
        o_ref[...] = res

    @pl.when(if_ref[n] == 0)
    def _():
        o_ref[...] += res


def _moe_gather(route, h2):
    n_rows = route["n_pos"] + MOE_BLK
    n_items = route["item_tile"].shape[0]
    grid_spec = pltpu.PrefetchScalarGridSpec(
        num_scalar_prefetch=4,
        grid=(n_items,),
        in_specs=[
            pl.BlockSpec((N_EXPERTS, MOE_SRC_BLK), lambda n, it, ib, fi, ie: (0, ib[n])),
            pl.BlockSpec((MOE_SRC_BLK, D_MODEL), lambda n, it, ib, fi, ie: (ib[n], 0)),
        ],
        out_specs=pl.BlockSpec((MOE_BLK, D_MODEL), lambda n, it, ib, fi, ie: (it[n], 0)),
    )
    return pl.pallas_call(
        _moe_gather_kernel,
        grid_spec=grid_spec,
        out_shape=jax.ShapeDtypeStruct((n_rows, D_MODEL), BF16),
        compiler_params=_cparams(("arbitrary",)),
        name="moe_gather",
    )(route["item_tile"], route["item_blk"], route["item_first"], route["item_expert"],
      route["pos_t"], h2)


def _moe_ffn_kernel(te_ref, ta_ref, x_ref, w1_ref, w3_ref, w2_ref, o_ref, acc_ref):
    k = pl.program_id(0)
    f = pl.program_id(1)
    nf = pl.num_programs(1)
    active = ta_ref[k] == 1

    @pl.when(active)
    def _():
        part = _swiglu_part(x_ref[...], w1_ref[...], w3_ref[...], w2_ref[...])

        @pl.when(f == 0)
        def _():
            acc_ref[...] = part

        @pl.when(f > 0)
        def _():
            acc_ref[...] += part

        @pl.when(f == nf - 1)
        def _():
            o_ref[...] = acc_ref[...].astype(o_ref.dtype)

    @pl.when(jnp.logical_not(active) & (f == nf - 1))
    def _():
        o_ref[...] = jnp.zeros_like(o_ref)


def _moe_ffn(route, xg, w1, w3, w2):
    n_pos = route["n_pos"]
    d_ff = w1.shape[2]
    grid_spec = pltpu.PrefetchScalarGridSpec(
        num_scalar_prefetch=2,
        grid=(n_pos // MOE_TILE, d_ff // MOE_FF_TILE),
        in_specs=[
            pl.BlockSpec((MOE_TILE, D_MODEL), lambda k, f, te, ta: (k, 0)),
            pl.BlockSpec((None, D_MODEL, MOE_FF_TILE), lambda k, f, te, ta: (te[k], 0, f)),
            pl.BlockSpec((None, D_MODEL, MOE_FF_TILE), lambda k, f, te, ta: (te[k], 0, f)),
            pl.BlockSpec((None, MOE_FF_TILE, D_MODEL), lambda k, f, te, ta: (te[k], f, 0)),
        ],
        out_specs=pl.BlockSpec((MOE_TILE, D_MODEL), lambda k, f, te, ta: (k, 0)),
        scratch_shapes=[pltpu.VMEM((MOE_TILE, D_MODEL), F32)],
    )
    return pl.pallas_call(
        _moe_ffn_kernel,
        grid_spec=grid_spec,
        out_shape=jax.ShapeDtypeStruct((n_pos, D_MODEL), BF16),
        compiler_params=_cparams(("arbitrary", "arbitrary")),
        name="moe_expert_ffn",
    )(route["tile_expert"], route["tile_active"], xg, w1, w3, w2)


def _moe_combine_kernel(ba_ref, x_ref, gate_ref, pos_ref, *refs):
    y_refs, o_ref = refs[:-1], refs[-1]
    tb = pl.program_id(0)
    lane = lax.broadcasted_iota(jnp.int32, (MOE_BLK, MOE_BLK), 1)
    out = x_ref[...]
    for e in range(N_EXPERTS):
        base = ba_ref[tb * N_EXPERTS + e] * MOE_BLK
        rel = pos_ref[:, e:e + 1] - base
        picked = None
        for half in range(2):
            onehot = jnp.where(rel - half * MOE_BLK == lane, 1.0, 0.0).astype(BF16)
            part = _dot(onehot, y_refs[2 * e + half][...])
            picked = part if picked is None else picked + part
        out = out + gate_ref[:, e:e + 1] * picked
    o_ref[...] = out


def _moe_combine(route, x1, gate, y):
    m = x1.shape[0]
    last_blk = route["n_pos"] // MOE_BLK - 1
    row = lambda tb, ba: (tb, 0)
    y_specs = []
    for e in range(N_EXPERTS):
        for half in range(2):
            y_specs.append(pl.BlockSpec(
                (MOE_BLK, D_MODEL),
                lambda tb, ba, e=e, half=half: (
                    jnp.minimum(ba[tb * N_EXPERTS + e] + half, last_blk), 0)))
    grid_spec = pltpu.PrefetchScalarGridSpec(
        num_scalar_prefetch=1,
        grid=(m // MOE_BLK,),
        in_specs=[
            pl.BlockSpec((MOE_BLK, D_MODEL), row),
            pl.BlockSpec((MOE_BLK, LANES), row),
            pl.BlockSpec((MOE_BLK, N_EXPERTS), row),
        ] + y_specs,
        out_specs=pl.BlockSpec((MOE_BLK, D_MODEL), row),
    )
    return pl.pallas_call(
        _moe_combine_kernel,
        grid_spec=grid_spec,
        out_shape=jax.ShapeDtypeStruct((m, D_MODEL), F32),
        compiler_params=_cparams(("arbitrary",)),
        name="moe_combine",
    )(route["blk_a"], x1, gate, route["pos"], *([y] * (2 * N_EXPERTS)))


def _moe(h2, x1, gate, w1, w3, w2):
    route = _moe_route(gate)
    xg = _moe_gather(route, h2)
    y = _moe_ffn(route, xg, w1, w3, w2)
    return _moe_combine(route, x1, gate, y)


def _pad_in_proj(w_in):
    c_q_end = 2560
    c_k, c_v_end = 2560, 2688
    c_qi, c_qi_end = 2688, 2944
    c_ki, c_wi_end = 2944, 3012
    c_kiw_end = c_ki + LANES
    w = w_in.astype(BF16)
    return jnp.concatenate(
        [w[:, :c_q_end], w[:, c_qi:c_qi_end], w[:, c_k:c_v_end], w[:, c_ki:c_kiw_end],
         w[:, c_wi_end:]], axis=1)


def _s5_params(lam_re, lam_im, log_dt, b_re, b_im, c_re, c_im):
    lr = jnp.minimum(lam_re.astype(F32), -1e-4)
    li = lam_im.astype(F32)
    dt = jnp.exp(log_dt.astype(F32))[:, None]
    mag = jnp.exp(lr * dt)
    ang = li * dt
    ab_re = mag * jnp.cos(ang)
    ab_im = mag * jnp.sin(ang)
    nr = ab_re - 1.0
    ni = ab_im
    den = lr * lr + li * li
    coef_re = (nr * lr + ni * li) / den
    coef_im = (ni * lr - nr * li) / den
    bf_re = coef_re[:, :, None] * b_re - coef_im[:, :, None] * b_im
    bf_im = coef_re[:, :, None] * b_im + coef_im[:, :, None] * b_re
    gh = SSM_GROUPS // SSM_HALVES
    eye = jnp.eye(gh, dtype=F32)

    def blockdiag_b(w):
        w = w.reshape(SSM_HALVES, gh, SSM_STATE, SSM_GROUP)
        return jnp.einsum('jgph,gk->jghkp', w, eye).reshape(SSM_HALVES, SSM_HALF_CH, SSM_HALF_STATES)

    def blockdiag_c(w):
        w = w.reshape(SSM_HALVES, gh, SSM_GROUP, SSM_STATE)
        return jnp.einsum('jghp,gk->jgpkh', w, eye).reshape(SSM_HALVES, SSM_HALF_STATES, SSM_HALF_CH)

    bc = jnp.concatenate([blockdiag_b(bf_re), blockdiag_b(bf_im)], axis=2).astype(BF16)
    cc = jnp.concatenate([blockdiag_c(c_re.astype(F32)), -blockdiag_c(c_im.astype(F32))],
                         axis=1).astype(BF16)
    a_re8 = jnp.broadcast_to(ab_re.reshape(1, -1), (SUBLANES, SSM_GROUPS * SSM_STATE))
    a_im8 = jnp.broadcast_to(ab_im.reshape(1, -1), (SUBLANES, SSM_GROUPS * SSM_STATE))
    return bc, a_re8, a_im8, cc


def _rope_tables(seq):
    pos = jnp.arange(seq, dtype=F32)
    inv = ROPE_THETA ** (-jnp.arange(HEAD_DIM // 2, dtype=F32) / (HEAD_DIM // 2))
    ang = pos[:, None] * inv[None, :]
    cos, sin = jnp.cos(ang), jnp.sin(ang)
    cos2 = jnp.concatenate([cos, cos], axis=-1)
    sin2 = jnp.concatenate([-sin, sin], axis=-1)
    one, zero = jnp.ones_like(cos2), jnp.zeros_like(sin2)
    tab_a = jnp.concatenate([cos2, cos2, sin2, sin2], axis=-1)
    tab_b = jnp.concatenate([cos2, one, sin2, zero], axis=-1)
    return tab_a, tab_b


def kernel(x, ln1_g, w_in, conv_w, ssm_lam_re, ssm_lam_im, ssm_log_dt, ssm_b_re, ssm_b_im,
           ssm_c_re, ssm_c_im, ssm_d, ssm_w_glu, q_norm_g, k_norm_g, w_out_a, w_out_b, w_out_c,
           w_o, ln2_g, ffn_w1, ffn_w3, ffn_w2, router_w, moe_w1, moe_w3, moe_w2):
    batch, seq, d = x.shape
    assert batch == SUBLANES and d == D_MODEL
    depth = w_in.shape[0]
    m = batch * seq
    tab_a, tab_b = _rope_tables(seq)
    ii = lax.broadcasted_iota(jnp.int32, (LANES, LANES), 0)
    jj = lax.broadcasted_iota(jnp.int32, (LANES, LANES), 1)
    tri = (ii < jj).astype(BF16)
    gsum = ((ii // HEAD_DIM) == (jj // HEAD_DIM)).astype(BF16)
    r_tm = lax.broadcasted_iota(jnp.int32, (MIX_ROWS, MIX_ROWS), 0)
    r_bm = lax.broadcasted_iota(jnp.int32, (MIX_ROWS, MIX_ROWS), 1)
    perm = ((r_tm % SUBLANES) * MIX_STEPS + r_tm // SUBLANES == r_bm).astype(BF16)
    perm_t = perm.T

    xt = x.reshape(m, d)
    for layer in range(depth):
        q_gain2 = jnp.tile(q_norm_g[layer], 2)[None, :]
        k_gain2 = jnp.tile(k_norm_g[layer], 2)[None, :]
        proj = _inproj(xt, ln1_g[layer][None, :], _pad_in_proj(w_in[layer]), tab_a, tab_b,
                       q_gain2, k_gain2, gsum, seq)
        proj3 = proj.reshape(batch, seq, PROJ_COLS)
        bc, a_re8, a_im8, cc = _s5_params(
            ssm_lam_re[layer], ssm_lam_im[layer], ssm_log_dt[layer], ssm_b_re[layer],
            ssm_b_im[layer], ssm_c_re[layer], ssm_c_im[layer])
        mab = _mixer(proj3, conv_w[layer], perm, perm_t, bc, a_re8, a_im8, cc,
                     ssm_d[layer][None, :], ssm_w_glu[layer].astype(BF16),
                     w_out_a[layer].astype(BF16), w_out_b[layer].astype(BF16))
        mab = mab.reshape(m, D_MODEL)
        attn = _dsa(proj3, tri).reshape(m, ATTN_WIDTH)
        j = layer // 2
        if layer % 2 == 0:
            x1, h2 = _merge(attn, mab, proj, xt, w_out_c[layer].astype(BF16),
                            w_o[layer].astype(BF16), ln2_g[layer][None, :], None)
            xt = _ffn(h2, x1, ffn_w1[j].astype(BF16), ffn_w3[j].astype(BF16),
                      ffn_w2[j].astype(BF16))
        else:
            rw = jnp.pad(router_w[j], ((0, 0), (0, LANES - N_EXPERTS))).astype(BF16)
            x1, h2, gate = _merge(attn, mab, proj, xt, w_out_c[layer].astype(BF16),
                                  w_o[layer].astype(BF16), ln2_g[layer][None, :], rw)
            xt = _moe(h2, x1, gate, moe_w1[j].astype(BF16), moe_w3[j].astype(BF16),
                      moe_w2[j].astype(BF16))
    return xt.reshape(batch, seq, d)
```

```python
import functools

import jax
import jax.numpy as jnp
from jax import lax
from jax.experimental import pallas as pl
from jax.experimental.pallas import tpu as pltpu

F32 = jnp.float32
BF16 = jnp.bfloat16

D_MODEL = 1024
CONV_WIDTH = 512
CONV_K = 3
SSM_WIDTH = 512
SSM_GROUP = 16
SSM_GROUPS = 32
SSM_STATE = 64
ATTN_HEADS = 8
HEAD_DIM = 64
ATTN_WIDTH = 512
IDX_HEADS = 4
IDX_DIM = 64
TOPK_MAX = 256
ROPE_THETA = 10000.0
N_EXPERTS = 8
EPS = 1e-6
LOG2_E = 1.4426950408889634

SUBLANES = 8
LANES = 128
VMEM_LIMIT_BYTES = 56 * 1024 * 1024

COL_CONV = 0
COL_SSM = 1536
COL_Q = 2048
COL_QI = 2560
COL_KV = 2816
COL_KIW = 2944
COL_GATES = 3072
PROJ_COLS = 6144

SSM_HALVES = 2
SSM_HALF_STATES = SSM_GROUPS // SSM_HALVES * SSM_STATE
SSM_HALF_CH = SSM_WIDTH // SSM_HALVES
SSM_COLS = 2 * SSM_GROUPS * SSM_STATE
SCAN_COLS = 512

BISECT_ITERS = 28
BISECT_UNROLL = 4


def _cparams(sem):
    return pltpu.CompilerParams(dimension_semantics=sem, vmem_limit_bytes=VMEM_LIMIT_BYTES)


def _sigmoid(x):
    return 1.0 / (1.0 + jnp.exp(-x))


def _dot(a, b):
    return jnp.dot(a, b, preferred_element_type=F32)


def _dot_nt(a, b):
    return lax.dot_general(a, b, (((1,), (1,)), ((), ())), preferred_element_type=F32)


INPROJ_ROWS = 512
INPROJ_NCHUNK = 512


def _rope_block(x, cos, sin):
    lane = lax.broadcasted_iota(jnp.int32, x.shape, 1)
    first_half = (lane % HEAD_DIM) < (HEAD_DIM // 2)
    swapped = jnp.where(first_half, pltpu.roll(x, LANES - HEAD_DIM // 2, 1),
                        pltpu.roll(x, HEAD_DIM // 2, 1))
    return x * cos + swapped * sin


def _head_inv_rms(x, gsum):
    sq = x * x
    hi = sq.astype(BF16)
    lo = (sq - hi.astype(F32)).astype(BF16)
    ss = _dot(hi, gsum) + _dot(lo, gsum)
    return lax.rsqrt(ss * (1.0 / HEAD_DIM) + EPS)


def _inproj_kernel(x_ref, g_ref, w_ref, taba_ref, tabb_ref, qg_ref, kg_ref, gsum_ref, o_ref):
    x = x_ref[...]
    ms = jnp.mean(x * x, axis=-1, keepdims=True)
    h = ((x * lax.rsqrt(ms + EPS)) * g_ref[...]).astype(BF16)
    gsum = gsum_ref[...]
    cos_a, sin_a = taba_ref[:, 0:LANES], taba_ref[:, LANES:2 * LANES]
    cos_b, sin_b = tabb_ref[:, 0:LANES], tabb_ref[:, LANES:2 * LANES]
    for n in range(PROJ_COLS // INPROJ_NCHUNK):
        c0 = n * INPROJ_NCHUNK
        res = _dot(h, w_ref[:, c0:c0 + INPROJ_NCHUNK])
        if c0 == COL_Q:
            for blk in range(INPROJ_NCHUNK // LANES):
                xb = res[:, blk * LANES:(blk + 1) * LANES]
                xb = (xb * _head_inv_rms(xb, gsum)) * qg_ref[...]
                xb = _rope_block(xb, cos_a, sin_a) * (HEAD_DIM ** -0.5 * LOG2_E)
                o_ref[:, c0 + blk * LANES:c0 + (blk + 1) * LANES] = xb.astype(o_ref.dtype)
        elif c0 == COL_QI:
            for blk in range(INPROJ_NCHUNK // LANES):
                xb = res[:, blk * LANES:(blk + 1) * LANES]
                col = c0 + blk * LANES
                if col < COL_KV:
                    xb = _rope_block(xb, cos_a, sin_a)
                elif col == COL_KV:
                    lane = lax.broadcasted_iota(jnp.int32, xb.shape, 1)
                    xn = (xb * _head_inv_rms(xb, gsum)) * kg_ref[...]
                    xb = _rope_block(jnp.where(lane < HEAD_DIM, xn, xb), cos_b, sin_b)
                else:
                    xb = _rope_block(xb, cos_b, sin_b)
                o_ref[:, col:col + LANES] = xb.astype(o_ref.dtype)
        else:
            o_ref[:, c0:c0 + INPROJ_NCHUNK] = res.astype(o_ref.dtype)


def _inproj(x2d, ln_g, w_in_p, tab_a, tab_b, q_gain2, k_gain2, gsum, seq):
    m = x2d.shape[0]
    tiles_per_seq = seq // INPROJ_ROWS
    const = lambda i: (0, 0)
    return pl.pallas_call(
        _inproj_kernel,
        grid=(m // INPROJ_ROWS,),
        in_specs=[
            pl.BlockSpec((INPROJ_ROWS, D_MODEL), lambda i: (i, 0)),
            pl.BlockSpec((1, D_MODEL), const),
            pl.BlockSpec((D_MODEL, PROJ_COLS), const),
            pl.BlockSpec((INPROJ_ROWS, 2 * LANES), lambda i: (i % tiles_per_seq, 0)),
            pl.BlockSpec((INPROJ_ROWS, 2 * LANES), lambda i: (i % tiles_per_seq, 0)),
            pl.BlockSpec((1, LANES), const),
            pl.BlockSpec((1, LANES), const),
            pl.BlockSpec((LANES, LANES), const),
        ],
        out_specs=pl.BlockSpec((INPROJ_ROWS, PROJ_COLS), lambda i: (i, 0)),
        out_shape=jax.ShapeDtypeStruct((m, PROJ_COLS), BF16),
        compiler_params=_cparams(("parallel",)),
        name="inproj",
    )(x2d, ln_g, w_in_p, tab_a, tab_b, q_gain2, k_gain2, gsum)


MIX_STEPS = 64
MIX_ROWS = MIX_STEPS * SUBLANES


def _mixer_kernel(pa_ref, su_ref, g0_ref, g1_ref, cw_ref, perm_ref, permt_ref, bc_ref, are_ref,
                  aim_ref, cc_ref, d_ref, wglu_ref, woa_ref, wob_ref, o_ref,
                  xs_ref, hst_ref, vext_ref):
    i = pl.program_id(0)
    rows = MIX_ROWS
    tc = MIX_STEPS

    @pl.when(i == 0)
    def _():
        hst_ref[...] = jnp.zeros_like(hst_ref)
        vext_ref[:, 0:SUBLANES, :] = jnp.zeros((SUBLANES, SUBLANES, CONV_WIDTH), F32)

    u = pa_ref[:, :, 0:CONV_WIDTH].astype(F32)
    bg = pa_ref[:, :, CONV_WIDTH:2 * CONV_WIDTH].astype(F32)
    cg = pa_ref[:, :, 2 * CONV_WIDTH:3 * CONV_WIDTH].astype(F32)
    v = cg * u
    vext_ref[:, SUBLANES:SUBLANES + tc, :] = v
    y = (cw_ref[0:1, :] * vext_ref[:, SUBLANES - 2:SUBLANES - 2 + tc, :]
         + cw_ref[1:2, :] * vext_ref[:, SUBLANES - 1:SUBLANES - 1 + tc, :]
         + cw_ref[2:3, :] * v)
    vext_ref[:, 0:SUBLANES, :] = v[:, tc - SUBLANES:tc, :]
    y_a = _dot((bg * y).reshape(rows, CONV_WIDTH).astype(BF16), woa_ref[...])

    u_tm = _dot(perm_ref[...], su_ref[...].reshape(rows, SSM_WIDTH))
    u_tm_b = u_tm.astype(BF16)
    for j in range(SSM_HALVES):
        xs_ref[:, j * 2 * SSM_HALF_STATES:(j + 1) * 2 * SSM_HALF_STATES] = _dot(
            u_tm_b[:, j * SSM_HALF_CH:(j + 1) * SSM_HALF_CH], bc_ref[j])

    for j in range(SSM_HALVES):
        for q in range(SSM_HALF_STATES // SCAN_COLS):
            cr = j * 2 * SSM_HALF_STATES + q * SCAN_COLS
            ci = cr + SSM_HALF_STATES
            ca = j * SSM_HALF_STATES + q * SCAN_COLS
            ar = are_ref[:, ca:ca + SCAN_COLS]
            ai = aim_ref[:, ca:ca + SCAN_COLS]

            def step(t, carry, cr=cr, ci=ci, ar=ar, ai=ai):
                hr, hi = carry
                r0 = pl.multiple_of(t * SUBLANES, SUBLANES)
                xr = xs_ref[pl.ds(r0, SUBLANES), cr:cr + SCAN_COLS]
                xi = xs_ref[pl.ds(r0, SUBLANES), ci:ci + SCAN_COLS]
                nr = ar * hr - ai * hi + xr
                ni = ar * hi + ai * hr + xi
                xs_ref[pl.ds(r0, SUBLANES), cr:cr + SCAN_COLS] = nr
                xs_ref[pl.ds(r0, SUBLANES), ci:ci + SCAN_COLS] = ni
                return nr, ni

            hr, hi = lax.fori_loop(
                0, MIX_STEPS, step,
                (hst_ref[:, cr:cr + SCAN_COLS], hst_ref[:, ci:ci + SCAN_COLS]), unroll=True)
            hst_ref[:, cr:cr + SCAN_COLS] = hr
            hst_ref[:, ci:ci + SCAN_COLS] = hi

    ys = []
    for j in range(SSM_HALVES):
        hj = xs_ref[:, j * 2 * SSM_HALF_STATES:(j + 1) * 2 * SSM_HALF_STATES].astype(BF16)
        ys.append(_dot(hj, cc_ref[j]))
    ysum = jnp.concatenate(ys, axis=-1) + d_ref[...] * u_tm
    yg = jax.nn.gelu(ysum, approximate=True)
    yg = (yg * _sigmoid(_dot(yg.astype(BF16), wglu_ref[...]))).astype(BF16)
    yg_bm = _dot(permt_ref[...], yg).astype(BF16)
    y_b = _dot(yg_bm, wob_ref[...])

    g0 = g0_ref[...].reshape(rows, D_MODEL).astype(F32)
    g1 = g1_ref[...].reshape(rows, D_MODEL).astype(F32)
    out = _sigmoid(g0) * y_a + _sigmoid(g1) * y_b
    o_ref[...] = out.reshape(SUBLANES, tc, D_MODEL).astype(o_ref.dtype)


def _mixer(proj3, conv_w, perm, perm_t, bc, a_re8, a_im8, cc, d_skip, w_glu, w_out_a, w_out_b):
    batch, seq, _ = proj3.shape
    const2 = lambda i: (0, 0)
    const3 = lambda i: (0, 0, 0)
    return pl.pallas_call(
        _mixer_kernel,
        grid=(seq // MIX_STEPS,),
        in_specs=[
            pl.BlockSpec((batch, MIX_STEPS, 3 * CONV_WIDTH),
                         lambda i: (0, i, COL_CONV // (3 * CONV_WIDTH))),
            pl.BlockSpec((batch, MIX_STEPS, SSM_WIDTH), lambda i: (0, i, COL_SSM // SSM_WIDTH)),
            pl.BlockSpec((batch, MIX_STEPS, D_MODEL), lambda i: (0, i, COL_GATES // D_MODEL)),
            pl.BlockSpec((batch, MIX_STEPS, D_MODEL), lambda i: (0, i, COL_GATES // D_MODEL + 1)),
            pl.BlockSpec((CONV_K, CONV_WIDTH), const2),
            pl.BlockSpec((MIX_ROWS, MIX_ROWS), const2),
            pl.BlockSpec((MIX_ROWS, MIX_ROWS), const2),
            pl.BlockSpec((SSM_HALVES, SSM_HALF_CH, 2 * SSM_HALF_STATES), const3),
            pl.BlockSpec((SUBLANES, SSM_GROUPS * SSM_STATE), const2),
            pl.BlockSpec((SUBLANES, SSM_GROUPS * SSM_STATE), const2),
            pl.BlockSpec((SSM_HALVES, 2 * SSM_HALF_STATES, SSM_HALF_CH), const3),
            pl.BlockSpec((1, SSM_WIDTH), const2),
            pl.BlockSpec((SSM_WIDTH, SSM_WIDTH), const2),
            pl.BlockSpec((CONV_WIDTH, D_MODEL), const2),
            pl.BlockSpec((SSM_WIDTH, D_MODEL), const2),
        ],
        out_specs=pl.BlockSpec((batch, MIX_STEPS, D_MODEL), lambda i: (0, i, 0)),
        out_shape=jax.ShapeDtypeStruct((batch, seq, D_MODEL), BF16),
        scratch_shapes=[
            pltpu.VMEM((MIX_ROWS, SSM_COLS), F32),
            pltpu.VMEM((SUBLANES, SSM_COLS), F32),
            pltpu.VMEM((SUBLANES, MIX_STEPS + SUBLANES, CONV_WIDTH), F32),
        ],
        compiler_params=_cparams(("arbitrary",)),
        name="conv_s5_mixer",
    )(proj3, proj3, proj3, proj3, conv_w, perm, perm_t, bc, a_re8, a_im8, cc, d_skip, w_glu,
      w_out_a, w_out_b)


DSA_TQ = 512
DSA_CLASSES = 4
DSA_CHAINS = 2
DSA_HEAD_GROUP = 4


def _dsa_kernel(q_ref, qi_ref, kv_ref, kiw_ref, kiwq_ref, tri_ref, o_ref,
                vext_ref, score_ref, bias_ref, *, tq, width, topk, row0):
    i = pl.program_id(1)
    rows_c = tq // DSA_CHAINS
    neg_inf = jnp.float32(-jnp.inf)
    pos_inf = jnp.float32(jnp.inf)

    @pl.when(i == 0)
    def _():
        lane = lax.broadcasted_iota(jnp.int32, (width, LANES), 1)
        shifted = pltpu.roll(kv_ref[...].astype(F32), HEAD_DIM, 1)
        vext_ref[...] = jnp.where(lane < HEAD_DIM, shifted, 1.0).astype(BF16)

    t0 = row0 + i * tq

    wq = kiwq_ref[:, IDX_DIM:IDX_DIM + IDX_HEADS].astype(F32) * (IDX_HEADS ** -0.5)
    kirot = kiw_ref[:, 0:IDX_DIM]
    score = None
    for h in range(IDX_HEADS):
        rel = jnp.maximum(_dot_nt(qi_ref[:, h * IDX_DIM:(h + 1) * IDX_DIM], kirot), 0.0)
        rel = rel * wq[:, h:h + 1]
        score = rel if score is None else score + rel
    col = lax.broadcasted_iota(jnp.int32, (tq, width), 1)
    row = t0 + lax.broadcasted_iota(jnp.int32, (tq, width), 0)
    score_ref[...] = jnp.where(col <= row, score, neg_inf)

    def chain_rows(c):
        return slice(c * rows_c, (c + 1) * rows_c)

    kks, los, his, fins = [], [], [], []
    for c in range(DSA_CHAINS):
        sc = score_ref[chain_rows(c), :]
        t_row = t0 + c * rows_c + lax.broadcasted_iota(jnp.int32, (rows_c, 1), 0)
        kk_c = jnp.minimum(t_row + 1, topk).astype(F32)
        hi_c = jnp.max(sc, axis=-1, keepdims=True)
        lo_c = jnp.min(jnp.where(sc == neg_inf, pos_inf, sc), axis=-1, keepdims=True)
        n_ge0 = jnp.sum(jnp.where(sc >= 0.0, 1.0, 0.0), axis=-1, keepdims=True)
        n_gt0 = jnp.sum(jnp.where(sc > 0.0, 1.0, 0.0), axis=-1, keepdims=True)
        above0 = n_gt0 >= kk_c
        reach0 = n_ge0 >= kk_c
        take_all = t_row < topk
        kks.append(kk_c)
        los.append(jnp.where(take_all, lo_c, jnp.where(reach0, 0.0, lo_c)))
        his.append(jnp.where(take_all, lo_c, jnp.where(above0, hi_c, 0.0)))
        fins.append(jnp.where(take_all | (reach0 & jnp.logical_not(above0)), 1.0, 0.0))

    def bisect_round(carry):
        it, los_c, his_c, fins_c, _ = carry
        open_rows = jnp.float32(0.0)
        for _ in range(BISECT_UNROLL):
            new_lo, new_hi, new_fin = [], [], []
            open_rows = jnp.float32(0.0)
            for c in range(DSA_CHAINS):
                sc = score_ref[chain_rows(c), :]
                mid = 0.5 * los_c[c] + 0.5 * his_c[c]
                cnt = jnp.sum(jnp.where(sc >= mid, 1.0, 0.0), axis=-1, keepdims=True)
                live = fins_c[c] == 0.0
                ge = cnt >= kks[c]
                hit = live & (cnt == kks[c])
                new_lo.append(jnp.where(live & ge, mid, los_c[c]))
                new_hi.append(jnp.where(live & (hit | jnp.logical_not(ge)), mid, his_c[c]))
                fin = jnp.where(hit, 1.0, fins_c[c])
                new_fin.append(fin)
                open_rows = jnp.maximum(open_rows, jnp.max(1.0 - fin))
            los_c, his_c, fins_c = tuple(new_lo), tuple(new_hi), tuple(new_fin)
        return it + BISECT_UNROLL, los_c, his_c, fins_c, open_rows

    def bisect_cond(carry):
        return (carry[0] < BISECT_ITERS) & (carry[4] > 0.0)

    _, los, _, _, _ = lax.while_loop(
        bisect_cond, bisect_round,
        (jnp.int32(0), tuple(los), tuple(his), tuple(fins), jnp.float32(1.0)))
    lo = jnp.concatenate(los, axis=0)
    kk = jnp.concatenate(kks, axis=0)

    def count_gt(thr):
        return jnp.sum(jnp.where(score_ref[...] > thr, 1.0, 0.0), axis=-1, keepdims=True)

    def snap_from(lo_v, strict):
        sc = score_ref[...]
        m = (sc > lo_v) if strict else (sc >= lo_v)
        return jnp.min(jnp.where(m, sc, pos_inf), axis=-1, keepdims=True)

    thr0 = snap_from(lo, False)
    above0 = count_gt(thr0)

    def refine_cond(carry):
        _, above = carry
        return jnp.max(jnp.where(above >= kk, 1.0, 0.0)) > 0.0

    def refine_body(carry):
        thr, above = carry
        thr = jnp.where(above >= kk, snap_from(thr, True), thr)
        return thr, count_gt(thr)

    thr, above = lax.while_loop(refine_cond, refine_body, (thr0, above0))

    need = kk - above
    ones_blk = jnp.ones((LANES, LANES), BF16)
    offs = jnp.zeros((tq, LANES), F32)
    for c in range(width // LANES):
        sl = slice(c * LANES, (c + 1) * LANES)
        sc = score_ref[:, sl]
        tie = sc == thr
        tie_b = jnp.where(tie, 1.0, 0.0).astype(BF16)
        before = _dot(tie_b, tri_ref[...]) + offs
        sel = (sc > thr) | (tie & (before < need))
        bias_ref[:, sl] = jnp.where(sel, 0.0, neg_inf)
        offs = offs + _dot(tie_b, ones_blk)

    krot = kv_ref[:, 0:HEAD_DIM]
    vext = vext_ref[...]
    for h0 in range(0, ATTN_HEADS, DSA_HEAD_GROUP):
        heads = range(h0, h0 + DSA_HEAD_GROUP)
        lgs = [_dot_nt(q_ref[:, h * HEAD_DIM:(h + 1) * HEAD_DIM], krot) + bias_ref[...]
               for h in heads]
        mxs = [jnp.max(lg, axis=-1, keepdims=True) for lg in lgs]
        ps = [jnp.exp2(lg - mx).astype(BF16) for lg, mx in zip(lgs, mxs)]
        ovs = [_dot(p, vext) for p in ps]
        for h, ov in zip(heads, ovs):
            oh = ov[:, 0:HEAD_DIM] / ov[:, HEAD_DIM:HEAD_DIM + 1]
            o_ref[:, h * HEAD_DIM:(h + 1) * HEAD_DIM] = oh.astype(o_ref.dtype)


def _dsa(proj3, tri):
    batch, seq, _ = proj3.shape
    topk = min(TOPK_MAX, seq // 4)
    class_len = seq // DSA_CLASSES
    tq = min(DSA_TQ, class_len)
    tiles = class_len // tq
    outs = []
    for c in range(DSA_CLASSES):
        width = (c + 1) * class_len
        base = c * tiles
        kern = functools.partial(_dsa_kernel, tq=tq, width=width, topk=topk,
                                 row0=c * class_len)
        outs.append(pl.pallas_call(
            kern,
            grid=(batch, tiles),
            in_specs=[
                pl.BlockSpec((None, tq, ATTN_WIDTH),
                             lambda b, i, base=base: (b, base + i, COL_Q // ATTN_WIDTH)),
                pl.BlockSpec((None, tq, IDX_HEADS * IDX_DIM),
                             lambda b, i, base=base: (b, base + i, COL_QI // (IDX_HEADS * IDX_DIM))),
                pl.BlockSpec((None, width, LANES), lambda b, i: (b, 0, COL_KV // LANES)),
                pl.BlockSpec((None, width, LANES), lambda b, i: (b, 0, COL_KIW // LANES)),
                pl.BlockSpec((None, tq, LANES),
                             lambda b, i, base=base: (b, base + i, COL_KIW // LANES)),
                pl.BlockSpec((LANES, LANES), lambda b, i: (0, 0)),
            ],
            out_specs=pl.BlockSpec((None, tq, ATTN_WIDTH), lambda b, i: (b, i, 0)),
            out_shape=jax.ShapeDtypeStruct((batch, class_len, ATTN_WIDTH), BF16),
            scratch_shapes=[
                pltpu.VMEM((width, LANES), BF16),
                pltpu.VMEM((tq, width), F32),
                pltpu.VMEM((tq, width), F32),
            ],
            compiler_params=_cparams(("arbitrary", "arbitrary")),
            name=f"dsa_attention_w{width}",
        )(proj3, proj3, proj3, proj3, proj3, tri))
    return jnp.concatenate(outs, axis=1)


MERGE_ROWS = 512


def _top2_gates(logits):
    lane = lax.broadcasted_iota(jnp.int32, logits.shape, 1)
    neg_inf = jnp.float32(-jnp.inf)
    lg = jnp.where(lane < N_EXPERTS, logits, neg_inf)
    v1 = jnp.max(lg, axis=-1, keepdims=True)
    i1 = jnp.min(jnp.where(lg == v1, lane, LANES), axis=-1, keepdims=True)
    rest = jnp.where(lane == i1, neg_inf, lg)
    v2 = jnp.max(rest, axis=-1, keepdims=True)
    i2 = jnp.min(jnp.where(rest == v2, lane, LANES), axis=-1, keepdims=True)
    e2 = jnp.exp(v2 - v1)
    den = 1.0 + e2
    return jnp.where(lane == i1, 1.0 / den, 0.0) + jnp.where(lane == i2, e2 / den, 0.0)


def _merge_kernel(*refs, with_router):
    if with_router:
        (at_ref, mab_ref, g2_ref, x_ref, woc_ref, wo_ref, ln_ref, rw_ref,
         x1_ref, h2_ref, gate_ref) = refs
    else:
        at_ref, mab_ref, g2_ref, x_ref, woc_ref, wo_ref, ln_ref, x1_ref, h2_ref = refs
    y_c = _dot(at_ref[...], woc_ref[...])
    mixed = mab_ref[...].astype(F32) + _sigmoid(g2_ref[...].astype(F32)) * y_c
    x1 = x_ref[...] + _dot(mixed.astype(BF16), wo_ref[...])
    x1_ref[...] = x1
    ms = jnp.mean(x1 * x1, axis=-1, keepdims=True)
    h2 = ((x1 * lax.rsqrt(ms + EPS)) * ln_ref[...]).astype(BF16)
    h2_ref[...] = h2
    if with_router:
        gate_ref[...] = _top2_gates(_dot(h2, rw_ref[...]))


def _merge(attn, mab, proj, x2d, w_out_c, w_o, ln2_g, router_w):
    m = x2d.shape[0]
    with_router = router_w is not None
    row = lambda i: (i, 0)
    const = lambda i: (0, 0)
    in_specs = [
        pl.BlockSpec((MERGE_ROWS, ATTN_WIDTH), row),
        pl.BlockSpec((MERGE_ROWS, D_MODEL), row),
        pl.BlockSpec((MERGE_ROWS, D_MODEL), lambda i: (i, COL_GATES // D_MODEL + 2)),
        pl.BlockSpec((MERGE_ROWS, D_MODEL), row),
        pl.BlockSpec((ATTN_WIDTH, D_MODEL), const),
        pl.BlockSpec((D_MODEL, D_MODEL), const),
        pl.BlockSpec((1, D_MODEL), const),
    ]
    out_specs = [pl.BlockSpec((MERGE_ROWS, D_MODEL), row), pl.BlockSpec((MERGE_ROWS, D_MODEL), row)]
    out_shape = [jax.ShapeDtypeStruct((m, D_MODEL), F32), jax.ShapeDtypeStruct((m, D_MODEL), BF16)]
    args = [attn, mab, proj, x2d, w_out_c, w_o, ln2_g]
    if with_router:
        in_specs.append(pl.BlockSpec((D_MODEL, LANES), const))
        out_specs.append(pl.BlockSpec((MERGE_ROWS, LANES), row))
        out_shape.append(jax.ShapeDtypeStruct((m, LANES), F32))
        args.append(router_w)
    return pl.pallas_call(
        functools.partial(_merge_kernel, with_router=with_router),
        grid=(m // MERGE_ROWS,),
        in_specs=in_specs,
        out_specs=out_specs,
        out_shape=out_shape,
        compiler_params=_cparams(("parallel",)),
        name="merge_out_router" if with_router else "merge_out",
    )(*args)


FFN_ROWS = 512
FFN_TILE = 1408


def _swiglu_part(h, w1, w3, w2):
    a = _dot(h, w1)
    b = _dot(h, w3)
    return _dot(((a * _sigmoid(a)) * b).astype(BF16), w2)


def _ffn_kernel(h_ref, x_ref, w1_ref, w3_ref, w2_ref, o_ref):
    @pl.when(pl.program_id(1) == 0)
    def _():
        o_ref[...] = x_ref[...]

    o_ref[...] += _swiglu_part(h_ref[...], w1_ref[...], w3_ref[...], w2_ref[...])


def _ffn(h2, x1, w1, w3, w2):
    m = x1.shape[0]
    d_ff = w1.shape[1]
    row = lambda i, f: (i, 0)
    return pl.pallas_call(
        _ffn_kernel,
        grid=(m // FFN_ROWS, d_ff // FFN_TILE),
        in_specs=[
            pl.BlockSpec((FFN_ROWS, D_MODEL), row),
            pl.BlockSpec((FFN_ROWS, D_MODEL), row),
            pl.BlockSpec((D_MODEL, FFN_TILE), lambda i, f: (0, f)),
            pl.BlockSpec((D_MODEL, FFN_TILE), lambda i, f: (0, f)),
            pl.BlockSpec((FFN_TILE, D_MODEL), lambda i, f: (f, 0)),
        ],
        out_specs=pl.BlockSpec((FFN_ROWS, D_MODEL), row),
        out_shape=jax.ShapeDtypeStruct((m, D_MODEL), F32),
        compiler_params=_cparams(("parallel", "arbitrary")),
        name="dense_ffn",
    )(h2, x1, w1, w3, w2)


MOE_TILE = 512
MOE_FF_TILE = 1792
MOE_BLK = 256
MOE_SRC_BLK = 1024


def _count_le(sorted_ends, v):
    return jnp.sum((sorted_ends[None, :] <= v[:, None]).astype(jnp.int32), axis=1)


def _moe_route(gate):
    m = gate.shape[0]
    i32 = jnp.int32
    sel = gate[:, :N_EXPERTS] > 0.0
    seli = sel.astype(i32)
    csum = jnp.cumsum(seli, axis=0)
    rank = csum - seli
    cnt = csum[-1]
    gsz = (cnt + MOE_TILE - 1) // MOE_TILE * MOE_TILE
    gend = jnp.cumsum(gsz)
    goff = gend - gsz
    n_pos = 2 * m + N_EXPERTS * MOE_TILE
    pos = jnp.where(sel, goff[None, :] + rank, -1).astype(i32)

    n_ft = n_pos // MOE_TILE
    t0 = jnp.arange(n_ft, dtype=i32) * MOE_TILE
    tile_expert = jnp.minimum(_count_le(gend, t0), N_EXPERTS - 1)
    tile_active = (t0 < gend[-1]).astype(i32)

    n_db = n_pos // MOE_BLK
    n_sb = m // MOE_SRC_BLK
    d0 = jnp.arange(n_db, dtype=i32) * MOE_BLK
    e_d = jnp.minimum(_count_le(gend, d0), N_EXPERTS - 1)
    r0 = d0 - goff[e_d]
    cnt_d = cnt[e_d]
    has = (d0 < gend[-1]) & (r0 < cnt_d)
    r_last = jnp.minimum(r0 + MOE_BLK, cnt_d) - 1
    cb = csum[MOE_SRC_BLK - 1::MOE_SRC_BLK].T[e_d]
    b_lo = jnp.sum((cb <= r0[:, None]).astype(i32), axis=1)
    b_hi = jnp.sum((cb <= r_last[:, None]).astype(i32), axis=1)
    b_lo = jnp.where(has, b_lo, 0)
    n_it = jnp.where(has, b_hi - b_lo + 1, 1)
    it_end = jnp.cumsum(n_it)
    it_start = it_end - n_it
    n_items = n_db + N_EXPERTS * n_sb
    n = jnp.arange(n_items, dtype=i32)
    live = n < it_end[-1]
    d = jnp.minimum(_count_le(it_end, n), n_db - 1)
    item_tile = jnp.where(live, d, n_db).astype(i32)
    item_blk = jnp.where(live, b_lo[d] + n - it_start[d], 0).astype(i32)
    item_first = jnp.where(live, n == it_start[d], True).astype(i32)
    item_expert = jnp.where(live, e_d[d], 0).astype(i32)

    before = jnp.concatenate([jnp.zeros((1, N_EXPERTS), i32), csum], axis=0)[0:m:MOE_BLK]
    blk_a = ((goff[None, :] + before) // MOE_BLK).astype(i32).reshape(-1)
    return dict(pos=pos, pos_t=pos.T, tile_expert=tile_expert, tile_active=tile_active,
                item_tile=item_tile, item_blk=item_blk, item_first=item_first,
                item_expert=item_expert, blk_a=blk_a, n_pos=n_pos)


def _moe_gather_kernel(it_ref, ib_ref, if_ref, ie_ref, post_ref, h_ref, o_ref):
    n = pl.program_id(0)
    rel = post_ref[pl.ds(ie_ref[n], 1), :] - it_ref[n] * MOE_BLK
    row = lax.broadcasted_iota(jnp.int32, (MOE_BLK, MOE_SRC_BLK), 0)
    onehot = jnp.where(rel == row, 1.0, 0.0).astype(BF16)
    res = _dot(onehot, h_ref[...]).astype(o_ref.dtype)

    @pl.when(if_ref[n] == 1)
    def _():
        o_ref[...] = res

    @pl.when(if_ref[n] == 0)
    def _():
        o_ref[...] += res


def _moe_gather(route, h2):
    n_rows = route["n_pos"] + MOE_BLK
    n_items = route["item_tile"].shape[0]
    grid_spec = pltpu.PrefetchScalarGridSpec(
        num_scalar_prefetch=4,
        grid=(n_items,),
        in_specs=[
            pl.BlockSpec((N_EXPERTS, MOE_SRC_BLK), lambda n, it, ib, fi, ie: (0, ib[n])),
            pl.BlockSpec((MOE_SRC_BLK, D_MODEL), lambda n, it, ib, fi, ie: (ib[n], 0)),
        ],
        out_specs=pl.BlockSpec((MOE_BLK, D_MODEL), lambda n, it, ib, fi, ie: (it[n], 0)),
    )
    return pl.pallas_call(
        _moe_gather_kernel,
        grid_spec=grid_spec,
        out_shape=jax.ShapeDtypeStruct((n_rows, D_MODEL), BF16),
        compiler_params=_cparams(("arbitrary",)),
        name="moe_gather",
    )(route["item_tile"], route["item_blk"], route["item_first"], route["item_expert"],
      route["pos_t"], h2)


def _moe_ffn_kernel(te_ref, ta_ref, x_ref, w1_ref, w3_ref, w2_ref, o_ref, acc_ref):
    k = pl.program_id(0)
    f = pl.program_id(1)
    nf = pl.num_programs(1)
    active = ta_ref[k] == 1

    @pl.when(active)
    def _():
        part = _swiglu_part(x_ref[...], w1_ref[...], w3_ref[...], w2_ref[...].astype(BF16))

        @pl.when(f == 0)
        def _():
            acc_ref[...] = part

        @pl.when(f > 0)
        def _():
            acc_ref[...] += part

        @pl.when(f == nf - 1)
        def _():
            o_ref[...] = acc_ref[...].astype(o_ref.dtype)

    @pl.when(jnp.logical_not(active) & (f == nf - 1))
    def _():
        o_ref[...] = jnp.zeros_like(o_ref)


def _moe_ffn(route, xg, w1, w3, w2):
    n_pos = route["n_pos"]
    d_ff = w1.shape[2]
    grid_spec = pltpu.PrefetchScalarGridSpec(
        num_scalar_prefetch=2,
        grid=(n_pos // MOE_TILE, d_ff // MOE_FF_TILE),
        in_specs=[
            pl.BlockSpec((MOE_TILE, D_MODEL), lambda k, f, te, ta: (k, 0)),
            pl.BlockSpec((None, D_MODEL, MOE_FF_TILE), lambda k, f, te, ta: (te[k], 0, f)),
            pl.BlockSpec((None, D_MODEL, MOE_FF_TILE), lambda k, f, te, ta: (te[k], 0, f)),
            pl.BlockSpec((None, MOE_FF_TILE, D_MODEL), lambda k, f, te, ta: (te[k], f, 0)),
        ],
        out_specs=pl.BlockSpec((MOE_TILE, D_MODEL), lambda k, f, te, ta: (k, 0)),
        scratch_shapes=[pltpu.VMEM((MOE_TILE, D_MODEL), F32)],
    )
    return pl.pallas_call(
        _moe_ffn_kernel,
        grid_spec=grid_spec,
        out_shape=jax.ShapeDtypeStruct((n_pos, D_MODEL), BF16),
        compiler_params=_cparams(("arbitrary", "arbitrary")),
        name="moe_expert_ffn",
    )(route["tile_expert"], route["tile_active"], xg, w1, w3, w2)


def _moe_combine_kernel(ba_ref, x_ref, gate_ref, pos_ref, *refs):
    y_refs, o_ref = refs[:-1], refs[-1]
    tb = pl.program_id(0)
    lane = lax.broadcasted_iota(jnp.int32, (MOE_BLK, MOE_BLK), 1)
    out = x_ref[...]
    for e in range(N_EXPERTS):
        base = ba_ref[tb * N_EXPERTS + e] * MOE_BLK
        rel = pos_ref[:, e:e + 1] - base
        picked = None
        for half in range(2):
            onehot = jnp.where(rel - half * MOE_BLK == lane, 1.0, 0.0).astype(BF16)
            part = _dot(onehot, y_refs[2 * e + half][...])
            picked = part if picked is None else picked + part
        out = out + gate_ref[:, e:e + 1] * picked
    o_ref[...] = out


def _moe_combine(route, x1, gate, y):
    m = x1.shape[0]
    last_blk = route["n_pos"] // MOE_BLK - 1
    row = lambda tb, ba: (tb, 0)
    y_specs = []
    for e in range(N_EXPERTS):
        for half in range(2):
            y_specs.append(pl.BlockSpec(
                (MOE_BLK, D_MODEL),
                lambda tb, ba, e=e, half=half: (
                    jnp.minimum(ba[tb * N_EXPERTS + e] + half, last_blk), 0)))
    grid_spec = pltpu.PrefetchScalarGridSpec(
        num_scalar_prefetch=1,
        grid=(m // MOE_BLK,),
        in_specs=[
            pl.BlockSpec((MOE_BLK, D_MODEL), row),
            pl.BlockSpec((MOE_BLK, LANES), row),
            pl.BlockSpec((MOE_BLK, N_EXPERTS), row),
        ] + y_specs,
        out_specs=pl.BlockSpec((MOE_BLK, D_MODEL), row),
    )
    return pl.pallas_call(
        _moe_combine_kernel,
        grid_spec=grid_spec,
        out_shape=jax.ShapeDtypeStruct((m, D_MODEL), F32),
        compiler_params=_cparams(("arbitrary",)),
        name="moe_combine",
    )(route["blk_a"], x1, gate, route["pos"], *([y] * (2 * N_EXPERTS)))


def _moe(h2, x1, gate, w1, w3, w2):
    route = _moe_route(gate)
    xg = _moe_gather(route, h2)
    y = _moe_ffn(route, xg, w1, w3, w2)
    return _moe_combine(route, x1, gate, y)


def _pad_in_proj(w_in):
    c_q_end = 2560
    c_k, c_v_end = 2560, 2688
    c_qi, c_qi_end = 2688, 2944
    c_ki, c_wi_end = 2944, 3012
    c_kiw_end = c_ki + LANES
    w = w_in.astype(BF16)
    return jnp.concatenate(
        [w[:, :c_q_end], w[:, c_qi:c_qi_end], w[:, c_k:c_v_end], w[:, c_ki:c_kiw_end],
         w[:, c_wi_end:]], axis=1)


def _s5_params(lam_re, lam_im, log_dt, b_re, b_im, c_re, c_im):
    lr = jnp.minimum(lam_re.astype(F32), -1e-4)
    li = lam_im.astype(F32)
    dt = jnp.exp(log_dt.astype(F32))[:, None]
    mag = jnp.exp(lr * dt)
    ang = li * dt
    ab_re = mag * jnp.cos(ang)
    ab_im = mag * jnp.sin(ang)
    nr = ab_re - 1.0
    ni = ab_im
    den = lr * lr + li * li
    coef_re = (nr * lr + ni * li) / den
    coef_im = (ni * lr - nr * li) / den
    bf_re = coef_re[:, :, None] * b_re - coef_im[:, :, None] * b_im
    bf_im = coef_re[:, :, None] * b_im + coef_im[:, :, None] * b_re
    gh = SSM_GROUPS // SSM_HALVES
    eye = jnp.eye(gh, dtype=F32)

    def blockdiag_b(w):
        w = w.reshape(SSM_HALVES, gh, SSM_STATE, SSM_GROUP)
        return jnp.einsum('jgph,gk->jghkp', w, eye).reshape(SSM_HALVES, SSM_HALF_CH, SSM_HALF_STATES)

    def blockdiag_c(w):
        w = w.reshape(SSM_HALVES, gh, SSM_GROUP, SSM_STATE)
        return jnp.einsum('jghp,gk->jgpkh', w, eye).reshape(SSM_HALVES, SSM_HALF_STATES, SSM_HALF_CH)

    bc = jnp.concatenate([blockdiag_b(bf_re), blockdiag_b(bf_im)], axis=2).astype(BF16)
    cc = jnp.concatenate([blockdiag_c(c_re.astype(F32)), -blockdiag_c(c_im.astype(F32))],
                         axis=1).astype(BF16)
    a_re8 = jnp.broadcast_to(ab_re.reshape(1, -1), (SUBLANES, SSM_GROUPS * SSM_STATE))
    a_im8 = jnp.broadcast_to(ab_im.reshape(1, -1), (SUBLANES, SSM_GROUPS * SSM_STATE))
    return bc, a_re8, a_im8, cc


def _rope_tables(seq):
    pos = jnp.arange(seq, dtype=F32)
    inv = ROPE_THETA ** (-jnp.arange(HEAD_DIM // 2, dtype=F32) / (HEAD_DIM // 2))
    ang = pos[:, None] * inv[None, :]
    cos, sin = jnp.cos(ang), jnp.sin(ang)
    cos2 = jnp.concatenate([cos, cos], axis=-1)
    sin2 = jnp.concatenate([-sin, sin], axis=-1)
    one, zero = jnp.ones_like(cos2), jnp.zeros_like(sin2)
    tab_a = jnp.concatenate([cos2, cos2, sin2, sin2], axis=-1)
    tab_b = jnp.concatenate([cos2, one, sin2, zero], axis=-1)
    return tab_a, tab_b


def kernel(x, ln1_g, w_in, conv_w, ssm_lam_re, ssm_lam_im, ssm_log_dt, ssm_b_re, ssm_b_im,
           ssm_c_re, ssm_c_im, ssm_d, ssm_w_glu, q_norm_g, k_norm_g, w_out_a, w_out_b, w_out_c,
           w_o, ln2_g, ffn_w1, ffn_w3, ffn_w2, router_w, moe_w1, moe_w3, moe_w2):
    batch, seq, d = x.shape
    assert batch == SUBLANES and d == D_MODEL
    depth = w_in.shape[0]
    m = batch * seq
    tab_a, tab_b = _rope_tables(seq)
    ii = lax.broadcasted_iota(jnp.int32, (LANES, LANES), 0)
    jj = lax.broadcasted_iota(jnp.int32, (LANES, LANES), 1)
    tri = (ii < jj).astype(BF16)
    gsum = ((ii // HEAD_DIM) == (jj // HEAD_DIM)).astype(BF16)
    r_tm = lax.broadcasted_iota(jnp.int32, (MIX_ROWS, MIX_ROWS), 0)
    r_bm = lax.broadcasted_iota(jnp.int32, (MIX_ROWS, MIX_ROWS), 1)
    perm = ((r_tm % SUBLANES) * MIX_STEPS + r_tm // SUBLANES == r_bm).astype(BF16)
    perm_t = perm.T

    xt = x.reshape(m, d)
    for layer in range(depth):
        q_gain2 = jnp.tile(q_norm_g[layer], 2)[None, :]
        k_gain2 = jnp.tile(k_norm_g[layer], 2)[None, :]
        proj = _inproj(xt, ln1_g[layer][None, :], _pad_in_proj(w_in[layer]), tab_a, tab_b,
                       q_gain2, k_gain2, gsum, seq)
        proj3 = proj.reshape(batch, seq, PROJ_COLS)
        bc, a_re8, a_im8, cc = _s5_params(
            ssm_lam_re[layer], ssm_lam_im[layer], ssm_log_dt[layer], ssm_b_re[layer],
            ssm_b_im[layer], ssm_c_re[layer], ssm_c_im[layer])
        mab = _mixer(proj3, conv_w[layer], perm, perm_t, bc, a_re8, a_im8, cc,
                     ssm_d[layer][None, :], ssm_w_glu[layer].astype(BF16),
                     w_out_a[layer].astype(BF16), w_out_b[layer].astype(BF16))
        mab = mab.reshape(m, D_MODEL)
        attn = _dsa(proj3, tri).reshape(m, ATTN_WIDTH)
        j = layer // 2
        if layer % 2 == 0:
            x1, h2 = _merge(attn, mab, proj, xt, w_out_c[layer].astype(BF16),
                            w_o[layer].astype(BF16), ln2_g[layer][None, :], None)
            xt = _ffn(h2, x1, ffn_w1[j].astype(BF16), ffn_w3[j].astype(BF16),
                      ffn_w2[j].astype(BF16))
        else:
            rw = jnp.pad(router_w[j], ((0, 0), (0, LANES - N_EXPERTS))).astype(BF16)
            x1, h2, gate = _merge(attn, mab, proj, xt, w_out_c[layer].astype(BF16),
                                  w_o[layer].astype(BF16), ln2_g[layer][None, :], rw)
            xt = _moe(h2, x1, gate, moe_w1[j].astype(BF16), moe_w3[j].astype(BF16), moe_w2[j])
    return xt.reshape(batch, seq, d)
```

```python
import functools

import jax
import jax.numpy as jnp
from jax import lax
from jax.experimental import pallas as pl
from jax.experimental.pallas import tpu as pltpu

F32 = jnp.float32
BF16 = jnp.bfloat16

D_MODEL = 1024
CONV_WIDTH = 512
CONV_K = 3
SSM_WIDTH = 512
SSM_GROUP = 16
SSM_GROUPS = 32
SSM_STATE = 64
ATTN_HEADS = 8
HEAD_DIM = 64
ATTN_WIDTH = 512
IDX_HEADS = 4
IDX_DIM = 64
TOPK_MAX = 256
ROPE_THETA = 10000.0
N_EXPERTS = 8
EPS = 1e-6
LOG2_E = 1.4426950408889634

SUBLANES = 8
LANES = 128
VMEM_LIMIT_BYTES = 56 * 1024 * 1024

COL_CONV = 0
COL_SSM = 1536
COL_Q = 2048
COL_QI = 2560
COL_KV = 2816
COL_KIW = 2944
COL_GATES = 3072
PROJ_COLS = 6144

SSM_HALVES = 2
SSM_HALF_STATES = SSM_GROUPS // SSM_HALVES * SSM_STATE
SSM_HALF_CH = SSM_WIDTH // SSM_HALVES
SSM_COLS = 2 * SSM_GROUPS * SSM_STATE
SCAN_COLS = 512

BISECT_ITERS = 28
BISECT_UNROLL = 4


def _cparams(sem):
    return pltpu.CompilerParams(dimension_semantics=sem, vmem_limit_bytes=VMEM_LIMIT_BYTES)


def _sigmoid(x):
    return 1.0 / (1.0 + jnp.exp(-x))


def _dot(a, b):
    return jnp.dot(a, b, preferred_element_type=F32)


def _dot_nt(a, b):
    return lax.dot_general(a, b, (((1,), (1,)), ((), ())), preferred_element_type=F32)


INPROJ_ROWS = 512
INPROJ_NCHUNK = 512


def _rope_block(x, cos, sin):
    lane = lax.broadcasted_iota(jnp.int32, x.shape, 1)
    first_half = (lane % HEAD_DIM) < (HEAD_DIM // 2)
    swapped = jnp.where(first_half, pltpu.roll(x, LANES - HEAD_DIM // 2, 1),
                        pltpu.roll(x, HEAD_DIM // 2, 1))
    return x * cos + swapped * sin


def _head_inv_rms(x, gsum):
    sq = x * x
    hi = sq.astype(BF16)
    lo = (sq - hi.astype(F32)).astype(BF16)
    ss = _dot(hi, gsum) + _dot(lo, gsum)
    return lax.rsqrt(ss * (1.0 / HEAD_DIM) + EPS)


def _inproj_kernel(x_ref, g_ref, w_ref, taba_ref, tabb_ref, qg_ref, kg_ref, gsum_ref, o_ref):
    x = x_ref[...]
    ms = jnp.mean(x * x, axis=-1, keepdims=True)
    h = ((x * lax.rsqrt(ms + EPS)) * g_ref[...]).astype(BF16)
    gsum = gsum_ref[...]
    cos_a, sin_a = taba_ref[:, 0:LANES], taba_ref[:, LANES:2 * LANES]
    cos_b, sin_b = tabb_ref[:, 0:LANES], tabb_ref[:, LANES:2 * LANES]
    for n in range(PROJ_COLS // INPROJ_NCHUNK):
        c0 = n * INPROJ_NCHUNK
        res = _dot(h, w_ref[:, c0:c0 + INPROJ_NCHUNK])
        if c0 == COL_Q:
            for blk in range(INPROJ_NCHUNK // LANES):
                xb = res[:, blk * LANES:(blk + 1) * LANES]
                xb = (xb * _head_inv_rms(xb, gsum)) * qg_ref[...]
                xb = _rope_block(xb, cos_a, sin_a) * (HEAD_DIM ** -0.5 * LOG2_E)
                o_ref[:, c0 + blk * LANES:c0 + (blk + 1) * LANES] = xb.astype(o_ref.dtype)
        elif c0 == COL_QI:
            for blk in range(INPROJ_NCHUNK // LANES):
                xb = res[:, blk * LANES:(blk + 1) * LANES]
                col = c0 + blk * LANES
                if col < COL_KV:
                    xb = _rope_block(xb, cos_a, sin_a)
                elif col == COL_KV:
                    lane = lax.broadcasted_iota(jnp.int32, xb.shape, 1)
                    xn = (xb * _head_inv_rms(xb, gsum)) * kg_ref[...]
                    xb = _rope_block(jnp.where(lane < HEAD_DIM, xn, xb), cos_b, sin_b)
                else:
                    xb = _rope_block(xb, cos_b, sin_b)
                o_ref[:, col:col + LANES] = xb.astype(o_ref.dtype)
        else:
            o_ref[:, c0:c0 + INPROJ_NCHUNK] = res.astype(o_ref.dtype)


def _inproj(x2d, ln_g, w_in_p, layer, tab_a, tab_b, q_gain2, k_gain2, gsum, seq):
    m = x2d.shape[0]
    tiles_per_seq = seq // INPROJ_ROWS
    const = lambda i: (0, 0)
    return pl.pallas_call(
        _inproj_kernel,
        grid=(m // INPROJ_ROWS,),
        in_specs=[
            pl.BlockSpec((INPROJ_ROWS, D_MODEL), lambda i: (i, 0)),
            pl.BlockSpec((1, D_MODEL), const),
            pl.BlockSpec((None, D_MODEL, PROJ_COLS), lambda i: (layer, 0, 0)),
            pl.BlockSpec((INPROJ_ROWS, 2 * LANES), lambda i: (i % tiles_per_seq, 0)),
            pl.BlockSpec((INPROJ_ROWS, 2 * LANES), lambda i: (i % tiles_per_seq, 0)),
            pl.BlockSpec((1, LANES), const),
            pl.BlockSpec((1, LANES), const),
            pl.BlockSpec((LANES, LANES), const),
        ],
        out_specs=pl.BlockSpec((INPROJ_ROWS, PROJ_COLS), lambda i: (i, 0)),
        out_shape=jax.ShapeDtypeStruct((m, PROJ_COLS), BF16),
        compiler_params=_cparams(("parallel",)),
        name="inproj",
    )(x2d, ln_g, w_in_p, tab_a, tab_b, q_gain2, k_gain2, gsum)


MIX_STEPS = 64
MIX_ROWS = MIX_STEPS * SUBLANES


def _mixer_kernel(pa_ref, su_ref, g0_ref, g1_ref, cw_ref, perm_ref, permt_ref, bc_ref, are_ref,
                  aim_ref, cc_ref, d_ref, wglu_ref, woa_ref, wob_ref, o_ref,
                  xs_ref, hst_ref, vext_ref):
    i = pl.program_id(0)
    rows = MIX_ROWS
    tc = MIX_STEPS

    @pl.when(i == 0)
    def _():
        hst_ref[...] = jnp.zeros_like(hst_ref)
        vext_ref[:, 0:SUBLANES, :] = jnp.zeros((SUBLANES, SUBLANES, CONV_WIDTH), F32)

    u = pa_ref[:, :, 0:CONV_WIDTH].astype(F32)
    bg = pa_ref[:, :, CONV_WIDTH:2 * CONV_WIDTH].astype(F32)
    cg = pa_ref[:, :, 2 * CONV_WIDTH:3 * CONV_WIDTH].astype(F32)
    v = cg * u
    vext_ref[:, SUBLANES:SUBLANES + tc, :] = v
    y = (cw_ref[0:1, :] * vext_ref[:, SUBLANES - 2:SUBLANES - 2 + tc, :]
         + cw_ref[1:2, :] * vext_ref[:, SUBLANES - 1:SUBLANES - 1 + tc, :]
         + cw_ref[2:3, :] * v)
    vext_ref[:, 0:SUBLANES, :] = v[:, tc - SUBLANES:tc, :]
    y_a = _dot((bg * y).reshape(rows, CONV_WIDTH).astype(BF16), woa_ref[...])

    u_tm = _dot(perm_ref[...], su_ref[...].reshape(rows, SSM_WIDTH))
    u_tm_b = u_tm.astype(BF16)
    for j in range(SSM_HALVES):
        xs_ref[:, j * 2 * SSM_HALF_STATES:(j + 1) * 2 * SSM_HALF_STATES] = _dot(
            u_tm_b[:, j * SSM_HALF_CH:(j + 1) * SSM_HALF_CH], bc_ref[j])

    for j in range(SSM_HALVES):
        for q in range(SSM_HALF_STATES // SCAN_COLS):
            cr = j * 2 * SSM_HALF_STATES + q * SCAN_COLS
            ci = cr + SSM_HALF_STATES
            ca = j * SSM_HALF_STATES + q * SCAN_COLS
            ar = are_ref[:, ca:ca + SCAN_COLS]
            ai = aim_ref[:, ca:ca + SCAN_COLS]

            def step(t, carry, cr=cr, ci=ci, ar=ar, ai=ai):
                hr, hi = carry
                r0 = pl.multiple_of(t * SUBLANES, SUBLANES)
                xr = xs_ref[pl.ds(r0, SUBLANES), cr:cr + SCAN_COLS]
                xi = xs_ref[pl.ds(r0, SUBLANES), ci:ci + SCAN_COLS]
                nr = ar * hr - ai * hi + xr
                ni = ar * hi + ai * hr + xi
                xs_ref[pl.ds(r0, SUBLANES), cr:cr + SCAN_COLS] = nr
                xs_ref[pl.ds(r0, SUBLANES), ci:ci + SCAN_COLS] = ni
                return nr, ni

            hr, hi = lax.fori_loop(
                0, MIX_STEPS, step,
                (hst_ref[:, cr:cr + SCAN_COLS], hst_ref[:, ci:ci + SCAN_COLS]), unroll=True)
            hst_ref[:, cr:cr + SCAN_COLS] = hr
            hst_ref[:, ci:ci + SCAN_COLS] = hi

    ys = []
    for j in range(SSM_HALVES):
        hj = xs_ref[:, j * 2 * SSM_HALF_STATES:(j + 1) * 2 * SSM_HALF_STATES].astype(BF16)
        ys.append(_dot(hj, cc_ref[j]))
    ysum = jnp.concatenate(ys, axis=-1) + d_ref[...] * u_tm
    yg = jax.nn.gelu(ysum, approximate=True)
    yg = (yg * _sigmoid(_dot(yg.astype(BF16), wglu_ref[...]))).astype(BF16)
    yg_bm = _dot(permt_ref[...], yg).astype(BF16)
    y_b = _dot(yg_bm, wob_ref[...])

    g0 = g0_ref[...].reshape(rows, D_MODEL).astype(F32)
    g1 = g1_ref[...].reshape(rows, D_MODEL).astype(F32)
    out = _sigmoid(g0) * y_a + _sigmoid(g1) * y_b
    o_ref[...] = out.reshape(SUBLANES, tc, D_MODEL).astype(o_ref.dtype)


def _mixer(proj3, conv_w, perm, perm_t, bc, a_re8, a_im8, cc, d_skip, w_glu, w_out_a, w_out_b):
    batch, seq, _ = proj3.shape
    const2 = lambda i: (0, 0)
    const3 = lambda i: (0, 0, 0)
    return pl.pallas_call(
        _mixer_kernel,
        grid=(seq // MIX_STEPS,),
        in_specs=[
            pl.BlockSpec((batch, MIX_STEPS, 3 * CONV_WIDTH),
                         lambda i: (0, i, COL_CONV // (3 * CONV_WIDTH))),
            pl.BlockSpec((batch, MIX_STEPS, SSM_WIDTH), lambda i: (0, i, COL_SSM // SSM_WIDTH)),
            pl.BlockSpec((batch, MIX_STEPS, D_MODEL), lambda i: (0, i, COL_GATES // D_MODEL)),
            pl.BlockSpec((batch, MIX_STEPS, D_MODEL), lambda i: (0, i, COL_GATES // D_MODEL + 1)),
            pl.BlockSpec((CONV_K, CONV_WIDTH), const2),
            pl.BlockSpec((MIX_ROWS, MIX_ROWS), const2),
            pl.BlockSpec((MIX_ROWS, MIX_ROWS), const2),
            pl.BlockSpec((SSM_HALVES, SSM_HALF_CH, 2 * SSM_HALF_STATES), const3),
            pl.BlockSpec((SUBLANES, SSM_GROUPS * SSM_STATE), const2),
            pl.BlockSpec((SUBLANES, SSM_GROUPS * SSM_STATE), const2),
            pl.BlockSpec((SSM_HALVES, 2 * SSM_HALF_STATES, SSM_HALF_CH), const3),
            pl.BlockSpec((1, SSM_WIDTH), const2),
            pl.BlockSpec((SSM_WIDTH, SSM_WIDTH), const2),
            pl.BlockSpec((CONV_WIDTH, D_MODEL), const2),
            pl.BlockSpec((SSM_WIDTH, D_MODEL), const2),
        ],
        out_specs=pl.BlockSpec((batch, MIX_STEPS, D_MODEL), lambda i: (0, i, 0)),
        out_shape=jax.ShapeDtypeStruct((batch, seq, D_MODEL), BF16),
        scratch_shapes=[
            pltpu.VMEM((MIX_ROWS, SSM_COLS), F32),
            pltpu.VMEM((SUBLANES, SSM_COLS), F32),
            pltpu.VMEM((SUBLANES, MIX_STEPS + SUBLANES, CONV_WIDTH), F32),
        ],
        compiler_params=_cparams(("arbitrary",)),
        name="conv_s5_mixer",
    )(proj3, proj3, proj3, proj3, conv_w, perm, perm_t, bc, a_re8, a_im8, cc, d_skip, w_glu,
      w_out_a, w_out_b)


DSA_TQ = 512
DSA_CLASSES = 4
DSA_CHAINS = 2
DSA_HEAD_GROUP = 4


def _dsa_kernel(q_ref, qi_ref, kv_ref, kiw_ref, kiwq_ref, tri_ref, o_ref,
                vext_ref, score_ref, bias_ref, *, tq, width, topk, row0):
    i = pl.program_id(1)
    rows_c = tq // DSA_CHAINS
    neg_inf = jnp.float32(-jnp.inf)
    pos_inf = jnp.float32(jnp.inf)

    @pl.when(i == 0)
    def _():
        lane = lax.broadcasted_iota(jnp.int32, (width, LANES), 1)
        shifted = pltpu.roll(kv_ref[...].astype(F32), HEAD_DIM, 1)
        vext_ref[...] = jnp.where(lane < HEAD_DIM, shifted, 1.0).astype(BF16)

    t0 = row0 + i * tq

    wq = kiwq_ref[:, IDX_DIM:IDX_DIM + IDX_HEADS].astype(F32) * (IDX_HEADS ** -0.5)
    kirot = kiw_ref[:, 0:IDX_DIM]
    score = None
    for h in range(IDX_HEADS):
        rel = jnp.maximum(_dot_nt(qi_ref[:, h * IDX_DIM:(h + 1) * IDX_DIM], kirot), 0.0)
        rel = rel * wq[:, h:h + 1]
        score = rel if score is None else score + rel
    col = lax.broadcasted_iota(jnp.int32, (tq, width), 1)
    row = t0 + lax.broadcasted_iota(jnp.int32, (tq, width), 0)
    score_ref[...] = jnp.where(col <= row, score, neg_inf)

    def chain_rows(c):
        return slice(c * rows_c, (c + 1) * rows_c)

    kks, los, his, fins = [], [], [], []
    for c in range(DSA_CHAINS):
        sc = score_ref[chain_rows(c), :]
        t_row = t0 + c * rows_c + lax.broadcasted_iota(jnp.int32, (rows_c, 1), 0)
        kk_c = jnp.minimum(t_row + 1, topk).astype(F32)
        hi_c = jnp.max(sc, axis=-1, keepdims=True)
        lo_c = jnp.min(jnp.where(sc == neg_inf, pos_inf, sc), axis=-1, keepdims=True)
        n_ge0 = jnp.sum(jnp.where(sc >= 0.0, 1.0, 0.0), axis=-1, keepdims=True)
        n_gt0 = jnp.sum(jnp.where(sc > 0.0, 1.0, 0.0), axis=-1, keepdims=True)
        above0 = n_gt0 >= kk_c
        reach0 = n_ge0 >= kk_c
        take_all = t_row < topk
        kks.append(kk_c)
        los.append(jnp.where(take_all, lo_c, jnp.where(reach0, 0.0, lo_c)))
        his.append(jnp.where(take_all, lo_c, jnp.where(above0, hi_c, 0.0)))
        fins.append(jnp.where(take_all | (reach0 & jnp.logical_not(above0)), 1.0, 0.0))

    def bisect_round(carry):
        it, los_c, his_c, fins_c, _ = carry
        open_rows = jnp.float32(0.0)
        for _ in range(BISECT_UNROLL):
            new_lo, new_hi, new_fin = [], [], []
            open_rows = jnp.float32(0.0)
            for c in range(DSA_CHAINS):
                sc = score_ref[chain_rows(c), :]
                mid = 0.5 * los_c[c] + 0.5 * his_c[c]
                cnt = jnp.sum(jnp.where(sc >= mid, 1.0, 0.0), axis=-1, keepdims=True)
                live = fins_c[c] == 0.0
                ge = cnt >= kks[c]
                hit = live & (cnt == kks[c])
                new_lo.append(jnp.where(live & ge, mid, los_c[c]))
                new_hi.append(jnp.where(live & (hit | jnp.logical_not(ge)), mid, his_c[c]))
                fin = jnp.where(hit, 1.0, fins_c[c])
                new_fin.append(fin)
                open_rows = jnp.maximum(open_rows, jnp.max(1.0 - fin))
            los_c, his_c, fins_c = tuple(new_lo), tuple(new_hi), tuple(new_fin)
        return it + BISECT_UNROLL, los_c, his_c, fins_c, open_rows

    def bisect_cond(carry):
        return (carry[0] < BISECT_ITERS) & (carry[4] > 0.0)

    _, los, _, _, _ = lax.while_loop(
        bisect_cond, bisect_round,
        (jnp.int32(0), tuple(los), tuple(his), tuple(fins), jnp.float32(1.0)))
    lo = jnp.concatenate(los, axis=0)
    kk = jnp.concatenate(kks, axis=0)

    def count_gt(thr):
        return jnp.sum(jnp.where(score_ref[...] > thr, 1.0, 0.0), axis=-1, keepdims=True)

    def snap_from(lo_v, strict):
        sc = score_ref[...]
        m = (sc > lo_v) if strict else (sc >= lo_v)
        return jnp.min(jnp.where(m, sc, pos_inf), axis=-1, keepdims=True)

    thr0 = snap_from(lo, False)
    above0 = count_gt(thr0)

    def refine_cond(carry):
        _, above = carry
        return jnp.max(jnp.where(above >= kk, 1.0, 0.0)) > 0.0

    def refine_body(carry):
        thr, above = carry
        thr = jnp.where(above >= kk, snap_from(thr, True), thr)
        return thr, count_gt(thr)

    thr, above = lax.while_loop(refine_cond, refine_body, (thr0, above0))

    need = kk - above
    ones_blk = jnp.ones((LANES, LANES), BF16)
    offs = jnp.zeros((tq, LANES), F32)
    for c in range(width // LANES):
        sl = slice(c * LANES, (c + 1) * LANES)
        sc = score_ref[:, sl]
        tie = sc == thr
        tie_b = jnp.where(tie, 1.0, 0.0).astype(BF16)
        before = _dot(tie_b, tri_ref[...]) + offs
        sel = (sc > thr) | (tie & (before < need))
        bias_ref[:, sl] = jnp.where(sel, 0.0, neg_inf)
        offs = offs + _dot(tie_b, ones_blk)

    krot = kv_ref[:, 0:HEAD_DIM]
    vext = vext_ref[...]
    for h0 in range(0, ATTN_HEADS, DSA_HEAD_GROUP):
        heads = range(h0, h0 + DSA_HEAD_GROUP)
        lgs = [_dot_nt(q_ref[:, h * HEAD_DIM:(h + 1) * HEAD_DIM], krot) + bias_ref[...]
               for h in heads]
        mxs = [jnp.max(lg, axis=-1, keepdims=True) for lg in lgs]
        ps = [jnp.exp2(lg - mx).astype(BF16) for lg, mx in zip(lgs, mxs)]
        ovs = [_dot(p, vext) for p in ps]
        for h, ov in zip(heads, ovs):
            oh = ov[:, 0:HEAD_DIM] / ov[:, HEAD_DIM:HEAD_DIM + 1]
            o_ref[:, h * HEAD_DIM:(h + 1) * HEAD_DIM] = oh.astype(o_ref.dtype)


def _dsa(proj3, tri):
    batch, seq, _ = proj3.shape
    topk = min(TOPK_MAX, seq // 4)
    class_len = seq // DSA_CLASSES
    tq = min(DSA_TQ, class_len)
    tiles = class_len // tq
    outs = []
    for c in range(DSA_CLASSES):
        width = (c + 1) * class_len
        base = c * tiles
        kern = functools.partial(_dsa_kernel, tq=tq, width=width, topk=topk,
                                 row0=c * class_len)
        outs.append(pl.pallas_call(
            kern,
            grid=(batch, tiles),
            in_specs=[
                pl.BlockSpec((None, tq, ATTN_WIDTH),
                             lambda b, i, base=base: (b, base + i, COL_Q // ATTN_WIDTH)),
                pl.BlockSpec((None, tq, IDX_HEADS * IDX_DIM),
                             lambda b, i, base=base: (b, base + i, COL_QI // (IDX_HEADS * IDX_DIM))),
                pl.BlockSpec((None, width, LANES), lambda b, i: (b, 0, COL_KV // LANES)),
                pl.BlockSpec((None, width, LANES), lambda b, i: (b, 0, COL_KIW // LANES)),
                pl.BlockSpec((None, tq, LANES),
                             lambda b, i, base=base: (b, base + i, COL_KIW // LANES)),
                pl.BlockSpec((LANES, LANES), lambda b, i: (0, 0)),
            ],
            out_specs=pl.BlockSpec((None, tq, ATTN_WIDTH), lambda b, i: (b, i, 0)),
            out_shape=jax.ShapeDtypeStruct((batch, class_len, ATTN_WIDTH), BF16),
            scratch_shapes=[
                pltpu.VMEM((width, LANES), BF16),
                pltpu.VMEM((tq, width), F32),
                pltpu.VMEM((tq, width), F32),
            ],
            compiler_params=_cparams(("arbitrary", "arbitrary")),
            name=f"dsa_attention_w{width}",
        )(proj3, proj3, proj3, proj3, proj3, tri))
    return jnp.concatenate(outs, axis=1)


MERGE_ROWS = 512


def _top2_gates(logits):
    lane = lax.broadcasted_iota(jnp.int32, logits.shape, 1)
    neg_inf = jnp.float32(-jnp.inf)
    lg = jnp.where(lane < N_EXPERTS, logits, neg_inf)
    v1 = jnp.max(lg, axis=-1, keepdims=True)
    i1 = jnp.min(jnp.where(lg == v1, lane, LANES), axis=-1, keepdims=True)
    rest = jnp.where(lane == i1, neg_inf, lg)
    v2 = jnp.max(rest, axis=-1, keepdims=True)
    i2 = jnp.min(jnp.where(rest == v2, lane, LANES), axis=-1, keepdims=True)
    e2 = jnp.exp(v2 - v1)
    den = 1.0 + e2
    return jnp.where(lane == i1, 1.0 / den, 0.0) + jnp.where(lane == i2, e2 / den, 0.0)


def _merge_kernel(*refs, with_router):
    if with_router:
        (at_ref, mab_ref, g2_ref, x_ref, woc_ref, wo_ref, ln_ref, rw_ref,
         x1_ref, h2_ref, gate_ref) = refs
    else:
        at_ref, mab_ref, g2_ref, x_ref, woc_ref, wo_ref, ln_ref, x1_ref, h2_ref = refs
    y_c = _dot(at_ref[...], woc_ref[...])
    mixed = mab_ref[...].astype(F32) + _sigmoid(g2_ref[...].astype(F32)) * y_c
    x1 = x_ref[...] + _dot(mixed.astype(BF16), wo_ref[...])
    x1_ref[...] = x1
    ms = jnp.mean(x1 * x1, axis=-1, keepdims=True)
    h2 = ((x1 * lax.rsqrt(ms + EPS)) * ln_ref[...]).astype(BF16)
    h2_ref[...] = h2
    if with_router:
        gate_ref[...] = _top2_gates(_dot(h2, rw_ref[...]))


def _merge(attn, mab, proj, x2d, w_out_c, w_o, ln2_g, router_w):
    m = x2d.shape[0]
    with_router = router_w is not None
    row = lambda i: (i, 0)
    const = lambda i: (0, 0)
    in_specs = [
        pl.BlockSpec((MERGE_ROWS, ATTN_WIDTH), row),
        pl.BlockSpec((MERGE_ROWS, D_MODEL), row),
        pl.BlockSpec((MERGE_ROWS, D_MODEL), lambda i: (i, COL_GATES // D_MODEL + 2)),
        pl.BlockSpec((MERGE_ROWS, D_MODEL), row),
        pl.BlockSpec((ATTN_WIDTH, D_MODEL), const),
        pl.BlockSpec((D_MODEL, D_MODEL), const),
        pl.BlockSpec((1, D_MODEL), const),
    ]
    out_specs = [pl.BlockSpec((MERGE_ROWS, D_MODEL), row), pl.BlockSpec((MERGE_ROWS, D_MODEL), row)]
    out_shape = [jax.ShapeDtypeStruct((m, D_MODEL), F32), jax.ShapeDtypeStruct((m, D_MODEL), BF16)]
    args = [attn, mab, proj, x2d, w_out_c, w_o, ln2_g]
    if with_router:
        in_specs.append(pl.BlockSpec((D_MODEL, LANES), const))
        out_specs.append(pl.BlockSpec((MERGE_ROWS, LANES), row))
        out_shape.append(jax.ShapeDtypeStruct((m, LANES), F32))
        args.append(router_w)
    return pl.pallas_call(
        functools.partial(_merge_kernel, with_router=with_router),
        grid=(m // MERGE_ROWS,),
        in_specs=in_specs,
        out_specs=out_specs,
        out_shape=out_shape,
        compiler_params=_cparams(("parallel",)),
        name="merge_out_router" if with_router else "merge_out",
    )(*args)


FFN_ROWS = 512
FFN_TILE = 1408


def _swiglu_part(h, w1, w3, w2):
    a = _dot(h, w1)
    b = _dot(h, w3)
    return _dot(((a * _sigmoid(a)) * b).astype(BF16), w2)


def _ffn_kernel(h_ref, x_ref, w1_ref, w3_ref, w2_ref, o_ref):
    @pl.when(pl.program_id(1) == 0)
    def _():
        o_ref[...] = x_ref[...]

    o_ref[...] += _swiglu_part(h_ref[...], w1_ref[...], w3_ref[...], w2_ref[...])


def _ffn(h2, x1, w1, w3, w2):
    m = x1.shape[0]
    d_ff = w1.shape[1]
    row = lambda i, f: (i, 0)
    return pl.pallas_call(
        _ffn_kernel,
        grid=(m // FFN_ROWS, d_ff // FFN_TILE),
        in_specs=[
            pl.BlockSpec((FFN_ROWS, D_MODEL), row),
            pl.BlockSpec((FFN_ROWS, D_MODEL), row),
            pl.BlockSpec((D_MODEL, FFN_TILE), lambda i, f: (0, f)),
            pl.BlockSpec((D_MODEL, FFN_TILE), lambda i, f: (0, f)),
            pl.BlockSpec((FFN_TILE, D_MODEL), lambda i, f: (f, 0)),
        ],
        out_specs=pl.BlockSpec((FFN_ROWS, D_MODEL), row),
        out_shape=jax.ShapeDtypeStruct((m, D_MODEL), F32),
        compiler_params=_cparams(("parallel", "arbitrary")),
        name="dense_ffn",
    )(h2, x1, w1, w3, w2)


MOE_TILE = 512
MOE_FF_TILE = 1792
MOE_BLK = 256
MOE_SRC_BLK = 1024


def _count_le(sorted_ends, v):
    return jnp.sum((sorted_ends[None, :] <= v[:, None]).astype(jnp.int32), axis=1)


def _moe_route(gate):
    m = gate.shape[0]
    i32 = jnp.int32
    sel = gate[:, :N_EXPERTS] > 0.0
    seli = sel.astype(i32)
    csum = jnp.cumsum(seli, axis=0)
    rank = csum - seli
    cnt = csum[-1]
    gsz = (cnt + MOE_TILE - 1) // MOE_TILE * MOE_TILE
    gend = jnp.cumsum(gsz)
    goff = gend - gsz
    n_pos = 2 * m + N_EXPERTS * MOE_TILE
    pos = jnp.where(sel, goff[None, :] + rank, -1).astype(i32)

    n_ft = n_pos // MOE_TILE
    t0 = jnp.arange(n_ft, dtype=i32) * MOE_TILE
    tile_expert = jnp.minimum(_count_le(gend, t0), N_EXPERTS - 1)
    tile_active = (t0 < gend[-1]).astype(i32)

    n_db = n_pos // MOE_BLK
    n_sb = m // MOE_SRC_BLK
    d0 = jnp.arange(n_db, dtype=i32) * MOE_BLK
    e_d = jnp.minimum(_count_le(gend, d0), N_EXPERTS - 1)
    r0 = d0 - goff[e_d]
    cnt_d = cnt[e_d]
    has = (d0 < gend[-1]) & (r0 < cnt_d)
    r_last = jnp.minimum(r0 + MOE_BLK, cnt_d) - 1
    cb = csum[MOE_SRC_BLK - 1::MOE_SRC_BLK].T[e_d]
    b_lo = jnp.sum((cb <= r0[:, None]).astype(i32), axis=1)
    b_hi = jnp.sum((cb <= r_last[:, None]).astype(i32), axis=1)
    b_lo = jnp.where(has, b_lo, 0)
    n_it = jnp.where(has, b_hi - b_lo + 1, 1)
    it_end = jnp.cumsum(n_it)
    it_start = it_end - n_it
    n_items = n_db + N_EXPERTS * n_sb
    n = jnp.arange(n_items, dtype=i32)
    live = n < it_end[-1]
    d = jnp.minimum(_count_le(it_end, n), n_db - 1)
    item_tile = jnp.where(live, d, n_db).astype(i32)
    item_blk = jnp.where(live, b_lo[d] + n - it_start[d], 0).astype(i32)
    item_first = jnp.where(live, n == it_start[d], True).astype(i32)
    item_expert = jnp.where(live, e_d[d], 0).astype(i32)

    before = jnp.concatenate([jnp.zeros((1, N_EXPERTS), i32), csum], axis=0)[0:m:MOE_BLK]
    blk_a = ((goff[None, :] + before) // MOE_BLK).astype(i32).reshape(-1)
    return dict(pos=pos, pos_t=pos.T, tile_expert=tile_expert, tile_active=tile_active,
                item_tile=item_tile, item_blk=item_blk, item_first=item_first,
                item_expert=item_expert, blk_a=blk_a, n_pos=n_pos)


def _moe_gather_kernel(it_ref, ib_ref, if_ref, ie_ref, post_ref, h_ref, o_ref):
    n = pl.program_id(0)
    rel = post_ref[pl.ds(ie_ref[n], 1), :] - it_ref[n] * MOE_BLK
    row = lax.broadcasted_iota(jnp.int32, (MOE_BLK, MOE_SRC_BLK), 0)
    onehot = jnp.where(rel == row, 1.0, 0.0).astype(BF16)
    res = _dot(onehot, h_ref[...]).astype(o_ref.dtype)

    @pl.when(if_ref[n] == 1)
    def _():
        o_ref[...] = res

    @pl.when(if_ref[n] == 0)
    def _():
        o_ref[...] += res


def _moe_gather(route, h2):
    n_rows = route["n_pos"] + MOE_BLK
    n_items = route["item_tile"].shape[0]
    grid_spec = pltpu.PrefetchScalarGridSpec(
        num_scalar_prefetch=4,
        grid=(n_items,),
        in_specs=[
            pl.BlockSpec((N_EXPERTS, MOE_SRC_BLK), lambda n, it, ib, fi, ie: (0, ib[n])),
            pl.BlockSpec((MOE_SRC_BLK, D_MODEL), lambda n, it, ib, fi, ie: (ib[n], 0)),
        ],
        out_specs=pl.BlockSpec((MOE_BLK, D_MODEL), lambda n, it, ib, fi, ie: (it[n], 0)),
    )
    return pl.pallas_call(
        _moe_gather_kernel,
        grid_spec=grid_spec,
        out_shape=jax.ShapeDtypeStruct((n_rows, D_MODEL), BF16),
        compiler_params=_cparams(("arbitrary",)),
        name="moe_gather",
    )(route["item_tile"], route["item_blk"], route["item_first"], route["item_expert"],
      route["pos_t"], h2)


def _moe_ffn_kernel(te_ref, ta_ref, x_ref, w1_ref, w3_ref, w2_ref, o_ref, acc_ref):
    k = pl.program_id(0)
    f = pl.program_id(1)
    nf = pl.num_programs(1)
    active = ta_ref[k] == 1

    @pl.when(active)
    def _():
        part = _swiglu_part(x_ref[...], w1_ref[...], w3_ref[...], w2_ref[...].astype(BF16))

        @pl.when(f == 0)
        def _():
            acc_ref[...] = part

        @pl.when(f > 0)
        def _():
            acc_ref[...] += part

        @pl.when(f == nf - 1)
        def _():
            o_ref[...] = acc_ref[...].astype(o_ref.dtype)

    @pl.when(jnp.logical_not(active) & (f == nf - 1))
    def _():
        o_ref[...] = jnp.zeros_like(o_ref)


def _moe_ffn(route, xg, w1, w3, w2):
    n_pos = route["n_pos"]
    d_ff = w1.shape[2]
    grid_spec = pltpu.PrefetchScalarGridSpec(
        num_scalar_prefetch=2,
        grid=(n_pos // MOE_TILE, d_ff // MOE_FF_TILE),
        in_specs=[
            pl.BlockSpec((MOE_TILE, D_MODEL), lambda k, f, te, ta: (k, 0)),
            pl.BlockSpec((None, D_MODEL, MOE_FF_TILE), lambda k, f, te, ta: (te[k], 0, f)),
            pl.BlockSpec((None, D_MODEL, MOE_FF_TILE), lambda k, f, te, ta: (te[k], 0, f)),
            pl.BlockSpec((None, MOE_FF_TILE, D_MODEL), lambda k, f, te, ta: (te[k], f, 0)),
        ],
        out_specs=pl.BlockSpec((MOE_TILE, D_MODEL), lambda k, f, te, ta: (k, 0)),
        scratch_shapes=[pltpu.VMEM((MOE_TILE, D_MODEL), F32)],
    )
    return pl.pallas_call(
        _moe_ffn_kernel,
        grid_spec=grid_spec,
        out_shape=jax.ShapeDtypeStruct((n_pos, D_MODEL), BF16),
        compiler_params=_cparams(("arbitrary", "arbitrary")),
        name="moe_expert_ffn",
    )(route["tile_expert"], route["tile_active"], xg, w1, w3, w2)


def _moe_combine_kernel(ba_ref, x_ref, gate_ref, pos_ref, *refs):
    y_refs, o_ref = refs[:-1], refs[-1]
    tb = pl.program_id(0)
    lane = lax.broadcasted_iota(jnp.int32, (MOE_BLK, MOE_BLK), 1)
    out = x_ref[...]
    for e in range(N_EXPERTS):
        base = ba_ref[tb * N_EXPERTS + e] * MOE_BLK
        rel = pos_ref[:, e:e + 1] - base
        picked = None
        for half in range(2):
            onehot = jnp.where(rel - half * MOE_BLK == lane, 1.0, 0.0).astype(BF16)
            part = _dot(onehot, y_refs[2 * e + half][...])
            picked = part if picked is None else picked + part
        out = out + gate_ref[:, e:e + 1] * picked
    o_ref[...] = out


def _moe_combine(route, x1, gate, y):
    m = x1.shape[0]
    last_blk = route["n_pos"] // MOE_BLK - 1
    row = lambda tb, ba: (tb, 0)
    y_specs = []
    for e in range(N_EXPERTS):
        for half in range(2):
            y_specs.append(pl.BlockSpec(
                (MOE_BLK, D_MODEL),
                lambda tb, ba, e=e, half=half: (
                    jnp.minimum(ba[tb * N_EXPERTS + e] + half, last_blk), 0)))
    grid_spec = pltpu.PrefetchScalarGridSpec(
        num_scalar_prefetch=1,
        grid=(m // MOE_BLK,),
        in_specs=[
            pl.BlockSpec((MOE_BLK, D_MODEL), row),
            pl.BlockSpec((MOE_BLK, LANES), row),
            pl.BlockSpec((MOE_BLK, N_EXPERTS), row),
        ] + y_specs,
        out_specs=pl.BlockSpec((MOE_BLK, D_MODEL), row),
    )
    return pl.pallas_call(
        _moe_combine_kernel,
        grid_spec=grid_spec,
        out_shape=jax.ShapeDtypeStruct((m, D_MODEL), F32),
        compiler_params=_cparams(("arbitrary",)),
        name="moe_combine",
    )(route["blk_a"], x1, gate, route["pos"], *([y] * (2 * N_EXPERTS)))


def _moe(h2, x1, gate, w1, w3, w2):
    route = _moe_route(gate)
    xg = _moe_gather(route, h2)
    y = _moe_ffn(route, xg, w1, w3, w2)
    return _moe_combine(route, x1, gate, y)


WPREP_ROWS = 256
IN_PROJ_PIECES = ((0, 0, 2560), (COL_QI, 2688, 256), (COL_KV, 2560, 128), (COL_KIW, 2944, 128),
                  (COL_GATES, 3012, 3072))


def _wprep_kernel(w_ref, o_ref):
    for dst, src, width in IN_PROJ_PIECES:
        o_ref[:, dst:dst + width] = w_ref[:, src:src + width].astype(o_ref.dtype)


def _pad_in_proj(w_in):
    depth, d, in_cols = w_in.shape
    return pl.pallas_call(
        _wprep_kernel,
        grid=(depth, d // WPREP_ROWS),
        in_specs=[pl.BlockSpec((None, WPREP_ROWS, in_cols), lambda l, i: (l, i, 0))],
        out_specs=pl.BlockSpec((None, WPREP_ROWS, PROJ_COLS), lambda l, i: (l, i, 0)),
        out_shape=jax.ShapeDtypeStruct((depth, d, PROJ_COLS), BF16),
        compiler_params=_cparams(("parallel", "parallel")),
        name="in_proj_weight_layout",
    )(w_in)


def _s5_params(lam_re, lam_im, log_dt, b_re, b_im, c_re, c_im):
    lr = jnp.minimum(lam_re.astype(F32), -1e-4)
    li = lam_im.astype(F32)
    dt = jnp.exp(log_dt.astype(F32))[:, None]
    mag = jnp.exp(lr * dt)
    ang = li * dt
    ab_re = mag * jnp.cos(ang)
    ab_im = mag * jnp.sin(ang)
    nr = ab_re - 1.0
    ni = ab_im
    den = lr * lr + li * li
    coef_re = (nr * lr + ni * li) / den
    coef_im = (ni * lr - nr * li) / den
    bf_re = coef_re[:, :, None] * b_re - coef_im[:, :, None] * b_im
    bf_im = coef_re[:, :, None] * b_im + coef_im[:, :, None] * b_re
    gh = SSM_GROUPS // SSM_HALVES
    eye = jnp.eye(gh, dtype=F32)

    def blockdiag_b(w):
        w = w.reshape(SSM_HALVES, gh, SSM_STATE, SSM_GROUP)
        return jnp.einsum('jgph,gk->jghkp', w, eye).reshape(SSM_HALVES, SSM_HALF_CH, SSM_HALF_STATES)

    def blockdiag_c(w):
        w = w.reshape(SSM_HALVES, gh, SSM_GROUP, SSM_STATE)
        return jnp.einsum('jghp,gk->jgpkh', w, eye).reshape(SSM_HALVES, SSM_HALF_STATES, SSM_HALF_CH)

    bc = jnp.concatenate([blockdiag_b(bf_re), blockdiag_b(bf_im)], axis=2).astype(BF16)
    cc = jnp.concatenate([blockdiag_c(c_re.astype(F32)), -blockdiag_c(c_im.astype(F32))],
                         axis=1).astype(BF16)
    a_re8 = jnp.broadcast_to(ab_re.reshape(1, -1), (SUBLANES, SSM_GROUPS * SSM_STATE))
    a_im8 = jnp.broadcast_to(ab_im.reshape(1, -1), (SUBLANES, SSM_GROUPS * SSM_STATE))
    return bc, a_re8, a_im8, cc


def _rope_tables(seq):
    pos = jnp.arange(seq, dtype=F32)
    inv = ROPE_THETA ** (-jnp.arange(HEAD_DIM // 2, dtype=F32) / (HEAD_DIM // 2))
    ang = pos[:, None] * inv[None, :]
    cos, sin = jnp.cos(ang), jnp.sin(ang)
    cos2 = jnp.concatenate([cos, cos], axis=-1)
    sin2 = jnp.concatenate([-sin, sin], axis=-1)
    one, zero = jnp.ones_like(cos2), jnp.zeros_like(sin2)
    tab_a = jnp.concatenate([cos2, cos2, sin2, sin2], axis=-1)
    tab_b = jnp.concatenate([cos2, one, sin2, zero], axis=-1)
    return tab_a, tab_b


def kernel(x, ln1_g, w_in, conv_w, ssm_lam_re, ssm_lam_im, ssm_log_dt, ssm_b_re, ssm_b_im,
           ssm_c_re, ssm_c_im, ssm_d, ssm_w_glu, q_norm_g, k_norm_g, w_out_a, w_out_b, w_out_c,
           w_o, ln2_g, ffn_w1, ffn_w3, ffn_w2, router_w, moe_w1, moe_w3, moe_w2):
    batch, seq, d = x.shape
    assert batch == SUBLANES and d == D_MODEL
    depth = w_in.shape[0]
    m = batch * seq
    tab_a, tab_b = _rope_tables(seq)
    ii = lax.broadcasted_iota(jnp.int32, (LANES, LANES), 0)
    jj = lax.broadcasted_iota(jnp.int32, (LANES, LANES), 1)
    tri = (ii < jj).astype(BF16)
    gsum = ((ii // HEAD_DIM) == (jj // HEAD_DIM)).astype(BF16)
    r_tm = lax.broadcasted_iota(jnp.int32, (MIX_ROWS, MIX_ROWS), 0)
    r_bm = lax.broadcasted_iota(jnp.int32, (MIX_ROWS, MIX_ROWS), 1)
    perm = ((r_tm % SUBLANES) * MIX_STEPS + r_tm // SUBLANES == r_bm).astype(BF16)
    perm_t = perm.T

    w_in_p = _pad_in_proj(w_in)
    xt = x.reshape(m, d)
    for layer in range(depth):
        q_gain2 = jnp.tile(q_norm_g[layer], 2)[None, :]
        k_gain2 = jnp.tile(k_norm_g[layer], 2)[None, :]
        proj = _inproj(xt, ln1_g[layer][None, :], w_in_p, layer, tab_a, tab_b,
                       q_gain2, k_gain2, gsum, seq)
        proj3 = proj.reshape(batch, seq, PROJ_COLS)
        bc, a_re8, a_im8, cc = _s5_params(
            ssm_lam_re[layer], ssm_lam_im[layer], ssm_log_dt[layer], ssm_b_re[layer],
            ssm_b_im[layer], ssm_c_re[layer], ssm_c_im[layer])
        mab = _mixer(proj3, conv_w[layer], perm, perm_t, bc, a_re8, a_im8, cc,
                     ssm_d[layer][None, :], ssm_w_glu[layer].astype(BF16),
                     w_out_a[layer].astype(BF16), w_out_b[layer].astype(BF16))
        mab = mab.reshape(m, D_MODEL)
        attn = _dsa(proj3, tri).reshape(m, ATTN_WIDTH)
        j = layer // 2
        if layer % 2 == 0:
            x1, h2 = _merge(attn, mab, proj, xt, w_out_c[layer].astype(BF16),
                            w_o[layer].astype(BF16), ln2_g[layer][None, :], None)
            xt = _ffn(h2, x1, ffn_w1[j].astype(BF16), ffn_w3[j].astype(BF16),
                      ffn_w2[j].astype(BF16))
        else:
            rw = jnp.pad(router_w[j], ((0, 0), (0, LANES - N_EXPERTS))).astype(BF16)
            x1, h2, gate = _merge(attn, mab, proj, xt, w_out_c[layer].astype(BF16),
                                  w_o[layer].astype(BF16), ln2_g[layer][None, :], rw)
            xt = _moe(h2, x1, gate, moe_w1[j].astype(BF16), moe_w3[j].astype(BF16), moe_w2[j])
    return xt.reshape(batch, seq, d)
```

```python
import functools

import jax
import jax.numpy as jnp
from jax import lax
from jax.experimental import pallas as pl
from jax.experimental.pallas import tpu as pltpu

F32 = jnp.float32
BF16 = jnp.bfloat16

D_MODEL = 1024
CONV_WIDTH = 512
CONV_K = 3
SSM_WIDTH = 512
SSM_GROUP = 16
SSM_GROUPS = 32
SSM_STATE = 64
ATTN_HEADS = 8
HEAD_DIM = 64
ATTN_WIDTH = 512
IDX_HEADS = 4
IDX_DIM = 64
TOPK_MAX = 256
ROPE_THETA = 10000.0
N_EXPERTS = 8
EPS = 1e-6
LOG2_E = 1.4426950408889634

SUBLANES = 8
LANES = 128
VMEM_LIMIT_BYTES = 56 * 1024 * 1024

COL_CONV = 0
COL_SSM = 1536
COL_Q = 2048
COL_QI = 2560
COL_KV = 2816
COL_KIW = 2944
COL_GATES = 3072
PROJ_COLS = 6144

SSM_HALVES = 2
SSM_HALF_STATES = SSM_GROUPS // SSM_HALVES * SSM_STATE
SSM_HALF_CH = SSM_WIDTH // SSM_HALVES
SSM_COLS = 2 * SSM_GROUPS * SSM_STATE
SCAN_COLS = 512

BISECT_ITERS = 28
BISECT_UNROLL = 4


def _cparams(sem):
    return pltpu.CompilerParams(dimension_semantics=sem, vmem_limit_bytes=VMEM_LIMIT_BYTES)


def _sigmoid(x):
    return 1.0 / (1.0 + jnp.exp(-x))


def _dot(a, b):
    return jnp.dot(a, b, preferred_element_type=F32)


def _dot_nt(a, b):
    return lax.dot_general(a, b, (((1,), (1,)), ((), ())), preferred_element_type=F32)


INPROJ_ROWS = 512
INPROJ_NCHUNK = 512


def _rope_block(x, cos, sin):
    lane = lax.broadcasted_iota(jnp.int32, x.shape, 1)
    first_half = (lane % HEAD_DIM) < (HEAD_DIM // 2)
    swapped = jnp.where(first_half, pltpu.roll(x, LANES - HEAD_DIM // 2, 1),
                        pltpu.roll(x, HEAD_DIM // 2, 1))
    return x * cos + swapped * sin


def _head_inv_rms(x, gsum):
    sq = x * x
    hi = sq.astype(BF16)
    lo = (sq - hi.astype(F32)).astype(BF16)
    ss = _dot(hi, gsum) + _dot(lo, gsum)
    return lax.rsqrt(ss * (1.0 / HEAD_DIM) + EPS)


def _inproj_kernel(x_ref, g_ref, w_ref, taba_ref, tabb_ref, qg_ref, kg_ref, gsum_ref, o_ref):
    x = x_ref[...]
    ms = jnp.mean(x * x, axis=-1, keepdims=True)
    h = ((x * lax.rsqrt(ms + EPS)) * g_ref[...]).astype(BF16)
    gsum = gsum_ref[...]
    cos_a, sin_a = taba_ref[:, 0:LANES], taba_ref[:, LANES:2 * LANES]
    cos_b, sin_b = tabb_ref[:, 0:LANES], tabb_ref[:, LANES:2 * LANES]
    for n in range(PROJ_COLS // INPROJ_NCHUNK):
        c0 = n * INPROJ_NCHUNK
        res = _dot(h, w_ref[:, c0:c0 + INPROJ_NCHUNK])
        if c0 == COL_Q:
            for blk in range(INPROJ_NCHUNK // LANES):
                xb = res[:, blk * LANES:(blk + 1) * LANES]
                xb = (xb * _head_inv_rms(xb, gsum)) * qg_ref[...]
                xb = _rope_block(xb, cos_a, sin_a) * (HEAD_DIM ** -0.5 * LOG2_E)
                o_ref[:, c0 + blk * LANES:c0 + (blk + 1) * LANES] = xb.astype(o_ref.dtype)
        elif c0 == COL_QI:
            for blk in range(INPROJ_NCHUNK // LANES):
                xb = res[:, blk * LANES:(blk + 1) * LANES]
                col = c0 + blk * LANES
                if col < COL_KV:
                    xb = _rope_block(xb, cos_a, sin_a)
                elif col == COL_KV:
                    lane = lax.broadcasted_iota(jnp.int32, xb.shape, 1)
                    xn = (xb * _head_inv_rms(xb, gsum)) * kg_ref[...]
                    xb = _rope_block(jnp.where(lane < HEAD_DIM, xn, xb), cos_b, sin_b)
                else:
                    xb = _rope_block(xb, cos_b, sin_b)
                o_ref[:, col:col + LANES] = xb.astype(o_ref.dtype)
        else:
            o_ref[:, c0:c0 + INPROJ_NCHUNK] = res.astype(o_ref.dtype)


def _inproj(x2d, ln_g, w_in_p, layer, tab_a, tab_b, q_gain2, k_gain2, gsum, seq):
    m = x2d.shape[0]
    tiles_per_seq = seq // INPROJ_ROWS
    const = lambda i: (0, 0)
    return pl.pallas_call(
        _inproj_kernel,
        grid=(m // INPROJ_ROWS,),
        in_specs=[
            pl.BlockSpec((INPROJ_ROWS, D_MODEL), lambda i: (i, 0)),
            pl.BlockSpec((1, D_MODEL), const),
            pl.BlockSpec((None, D_MODEL, PROJ_COLS), lambda i: (layer, 0, 0)),
            pl.BlockSpec((INPROJ_ROWS, 2 * LANES), lambda i: (i % tiles_per_seq, 0)),
            pl.BlockSpec((INPROJ_ROWS, 2 * LANES), lambda i: (i % tiles_per_seq, 0)),
            pl.BlockSpec((1, LANES), const),
            pl.BlockSpec((1, LANES), const),
            pl.BlockSpec((LANES, LANES), const),
        ],
        out_specs=pl.BlockSpec((INPROJ_ROWS, PROJ_COLS), lambda i: (i, 0)),
        out_shape=jax.ShapeDtypeStruct((m, PROJ_COLS), BF16),
        compiler_params=_cparams(("parallel",)),
        name="inproj",
    )(x2d, ln_g, w_in_p, tab_a, tab_b, q_gain2, k_gain2, gsum)


MIX_STEPS = 64
MIX_ROWS = MIX_STEPS * SUBLANES


def _mixer_kernel(pa_ref, su_ref, g0_ref, g1_ref, cw_ref, perm_ref, permt_ref, bc_ref, are_ref,
                  aim_ref, cc_ref, d_ref, wglu_ref, woa_ref, wob_ref, o_ref,
                  xs_ref, hst_ref, vext_ref):
    i = pl.program_id(0)
    rows = MIX_ROWS
    tc = MIX_STEPS

    @pl.when(i == 0)
    def _():
        hst_ref[...] = jnp.zeros_like(hst_ref)
        vext_ref[:, 0:SUBLANES, :] = jnp.zeros((SUBLANES, SUBLANES, CONV_WIDTH), F32)

    u = pa_ref[:, :, 0:CONV_WIDTH].astype(F32)
    bg = pa_ref[:, :, CONV_WIDTH:2 * CONV_WIDTH].astype(F32)
    cg = pa_ref[:, :, 2 * CONV_WIDTH:3 * CONV_WIDTH].astype(F32)
    v = cg * u
    vext_ref[:, SUBLANES:SUBLANES + tc, :] = v
    y = (cw_ref[0:1, :] * vext_ref[:, SUBLANES - 2:SUBLANES - 2 + tc, :]
         + cw_ref[1:2, :] * vext_ref[:, SUBLANES - 1:SUBLANES - 1 + tc, :]
         + cw_ref[2:3, :] * v)
    vext_ref[:, 0:SUBLANES, :] = v[:, tc - SUBLANES:tc, :]
    y_a = _dot((bg * y).reshape(rows, CONV_WIDTH).astype(BF16), woa_ref[...])

    u_tm = _dot(perm_ref[...], su_ref[...].reshape(rows, SSM_WIDTH))
    u_tm_b = u_tm.astype(BF16)
    for j in range(SSM_HALVES):
        xs_ref[:, j * 2 * SSM_HALF_STATES:(j + 1) * 2 * SSM_HALF_STATES] = _dot(
            u_tm_b[:, j * SSM_HALF_CH:(j + 1) * SSM_HALF_CH], bc_ref[j])

    for j in range(SSM_HALVES):
        for q in range(SSM_HALF_STATES // SCAN_COLS):
            cr = j * 2 * SSM_HALF_STATES + q * SCAN_COLS
            ci = cr + SSM_HALF_STATES
            ca = j * SSM_HALF_STATES + q * SCAN_COLS
            ar = are_ref[:, ca:ca + SCAN_COLS]
            ai = aim_ref[:, ca:ca + SCAN_COLS]

            def step(t, carry, cr=cr, ci=ci, ar=ar, ai=ai):
                hr, hi = carry
                r0 = pl.multiple_of(t * SUBLANES, SUBLANES)
                xr = xs_ref[pl.ds(r0, SUBLANES), cr:cr + SCAN_COLS]
                xi = xs_ref[pl.ds(r0, SUBLANES), ci:ci + SCAN_COLS]
                nr = ar * hr - ai * hi + xr
                ni = ar * hi + ai * hr + xi
                xs_ref[pl.ds(r0, SUBLANES), cr:cr + SCAN_COLS] = nr
                xs_ref[pl.ds(r0, SUBLANES), ci:ci + SCAN_COLS] = ni
                return nr, ni

            hr, hi = lax.fori_loop(
                0, MIX_STEPS, step,
                (hst_ref[:, cr:cr + SCAN_COLS], hst_ref[:, ci:ci + SCAN_COLS]), unroll=True)
            hst_ref[:, cr:cr + SCAN_COLS] = hr
            hst_ref[:, ci:ci + SCAN_COLS] = hi

    ys = []
    for j in range(SSM_HALVES):
        hj = xs_ref[:, j * 2 * SSM_HALF_STATES:(j + 1) * 2 * SSM_HALF_STATES].astype(BF16)
        ys.append(_dot(hj, cc_ref[j]))
    ysum = jnp.concatenate(ys, axis=-1) + d_ref[...] * u_tm
    yg = jax.nn.gelu(ysum, approximate=True)
    yg = (yg * _sigmoid(_dot(yg.astype(BF16), wglu_ref[...]))).astype(BF16)
    yg_bm = _dot(permt_ref[...], yg).astype(BF16)
    y_b = _dot(yg_bm, wob_ref[...])

    g0 = g0_ref[...].reshape(rows, D_MODEL).astype(F32)
    g1 = g1_ref[...].reshape(rows, D_MODEL).astype(F32)
    out = _sigmoid(g0) * y_a + _sigmoid(g1) * y_b
    o_ref[...] = out.reshape(SUBLANES, tc, D_MODEL).astype(o_ref.dtype)


def _mixer(proj3, conv_w, perm, perm_t, bc, a_re8, a_im8, cc, d_skip, w_glu, w_out_a, w_out_b):
    batch, seq, _ = proj3.shape
    const2 = lambda i: (0, 0)
    const3 = lambda i: (0, 0, 0)
    return pl.pallas_call(
        _mixer_kernel,
        grid=(seq // MIX_STEPS,),
        in_specs=[
            pl.BlockSpec((batch, MIX_STEPS, 3 * CONV_WIDTH),
                         lambda i: (0, i, COL_CONV // (3 * CONV_WIDTH))),
            pl.BlockSpec((batch, MIX_STEPS, SSM_WIDTH), lambda i: (0, i, COL_SSM // SSM_WIDTH)),
            pl.BlockSpec((batch, MIX_STEPS, D_MODEL), lambda i: (0, i, COL_GATES // D_MODEL)),
            pl.BlockSpec((batch, MIX_STEPS, D_MODEL), lambda i: (0, i, COL_GATES // D_MODEL + 1)),
            pl.BlockSpec((CONV_K, CONV_WIDTH), const2),
            pl.BlockSpec((MIX_ROWS, MIX_ROWS), const2),
            pl.BlockSpec((MIX_ROWS, MIX_ROWS), const2),
            pl.BlockSpec((SSM_HALVES, SSM_HALF_CH, 2 * SSM_HALF_STATES), const3),
            pl.BlockSpec((SUBLANES, SSM_GROUPS * SSM_STATE), const2),
            pl.BlockSpec((SUBLANES, SSM_GROUPS * SSM_STATE), const2),
            pl.BlockSpec((SSM_HALVES, 2 * SSM_HALF_STATES, SSM_HALF_CH), const3),
            pl.BlockSpec((1, SSM_WIDTH), const2),
            pl.BlockSpec((SSM_WIDTH, SSM_WIDTH), const2),
            pl.BlockSpec((CONV_WIDTH, D_MODEL), const2),
            pl.BlockSpec((SSM_WIDTH, D_MODEL), const2),
        ],
        out_specs=pl.BlockSpec((batch, MIX_STEPS, D_MODEL), lambda i: (0, i, 0)),
        out_shape=jax.ShapeDtypeStruct((batch, seq, D_MODEL), BF16),
        scratch_shapes=[
            pltpu.VMEM((MIX_ROWS, SSM_COLS), F32),
            pltpu.VMEM((SUBLANES, SSM_COLS), F32),
            pltpu.VMEM((SUBLANES, MIX_STEPS + SUBLANES, CONV_WIDTH), F32),
        ],
        compiler_params=_cparams(("arbitrary",)),
        name="conv_s5_mixer",
    )(proj3, proj3, proj3, proj3, conv_w, perm, perm_t, bc, a_re8, a_im8, cc, d_skip, w_glu,
      w_out_a, w_out_b)


DSA_TQ = 512
DSA_CLASSES = 4
DSA_CHAINS = 2
DSA_HEAD_GROUP = 4


def _dsa_kernel(q_ref, qi_ref, kv_ref, kiw_ref, kiwq_ref, tri_ref, o_ref,
                vext_ref, score_ref, bias_ref, *, tq, width, topk, row0):
    i = pl.program_id(1)
    rows_c = tq // DSA_CHAINS
    neg_inf = jnp.float32(-jnp.inf)
    pos_inf = jnp.float32(jnp.inf)

    @pl.when(i == 0)
    def _():
        lane = lax.broadcasted_iota(jnp.int32, (width, LANES), 1)
        shifted = pltpu.roll(kv_ref[...].astype(F32), HEAD_DIM, 1)
        vext_ref[...] = jnp.where(lane < HEAD_DIM, shifted, 1.0).astype(BF16)

    t0 = row0 + i * tq

    wq = kiwq_ref[:, IDX_DIM:IDX_DIM + IDX_HEADS].astype(F32) * (IDX_HEADS ** -0.5)
    kirot = kiw_ref[:, 0:IDX_DIM]
    score = None
    for h in range(IDX_HEADS):
        rel = jnp.maximum(_dot_nt(qi_ref[:, h * IDX_DIM:(h + 1) * IDX_DIM], kirot), 0.0)
        rel = rel * wq[:, h:h + 1]
        score = rel if score is None else score + rel
    col = lax.broadcasted_iota(jnp.int32, (tq, width), 1)
    row = t0 + lax.broadcasted_iota(jnp.int32, (tq, width), 0)
    score_ref[...] = jnp.where(col <= row, score, neg_inf)

    def chain_rows(c):
        return slice(c * rows_c, (c + 1) * rows_c)

    kks, los, his, fins = [], [], [], []
    for c in range(DSA_CHAINS):
        sc = score_ref[chain_rows(c), :]
        t_row = t0 + c * rows_c + lax.broadcasted_iota(jnp.int32, (rows_c, 1), 0)
        kk_c = jnp.minimum(t_row + 1, topk).astype(F32)
        hi_c = jnp.max(sc, axis=-1, keepdims=True)
        lo_c = jnp.min(jnp.where(sc == neg_inf, pos_inf, sc), axis=-1, keepdims=True)
        n_ge0 = jnp.sum(jnp.where(sc >= 0.0, 1.0, 0.0), axis=-1, keepdims=True)
        n_gt0 = jnp.sum(jnp.where(sc > 0.0, 1.0, 0.0), axis=-1, keepdims=True)
        above0 = n_gt0 >= kk_c
        reach0 = n_ge0 >= kk_c
        take_all = t_row < topk
        kks.append(kk_c)
        los.append(jnp.where(take_all, lo_c, jnp.where(reach0, 0.0, lo_c)))
        his.append(jnp.where(take_all, lo_c, jnp.where(above0, hi_c, 0.0)))
        tie0 = reach0 & jnp.logical_not(above0)
        fins.append(jnp.where(take_all | (tie0 & (n_ge0 == kk_c)), 1.0,
                              jnp.where(tie0, 2.0, 0.0)))

    def bisect_round(carry):
        it, los_c, his_c, fins_c, _ = carry
        open_rows = jnp.float32(0.0)
        for _ in range(BISECT_UNROLL):
            new_lo, new_hi, new_fin = [], [], []
            open_rows = jnp.float32(0.0)
            for c in range(DSA_CHAINS):
                sc = score_ref[chain_rows(c), :]
                mid = 0.5 * los_c[c] + 0.5 * his_c[c]
                cnt = jnp.sum(jnp.where(sc >= mid, 1.0, 0.0), axis=-1, keepdims=True)
                live = fins_c[c] == 0.0
                ge = cnt >= kks[c]
                hit = live & (cnt == kks[c])
                new_lo.append(jnp.where(live & ge, mid, los_c[c]))
                new_hi.append(jnp.where(live & (hit | jnp.logical_not(ge)), mid, his_c[c]))
                fin = jnp.where(hit, 1.0, fins_c[c])
                new_fin.append(fin)
                open_rows = jnp.maximum(open_rows, jnp.max(jnp.where(fin == 0.0, 1.0, 0.0)))
            los_c, his_c, fins_c = tuple(new_lo), tuple(new_hi), tuple(new_fin)
        return it + BISECT_UNROLL, los_c, his_c, fins_c, open_rows

    def bisect_cond(carry):
        return (carry[0] < BISECT_ITERS) & (carry[4] > 0.0)

    _, los, _, fins, _ = lax.while_loop(
        bisect_cond, bisect_round,
        (jnp.int32(0), tuple(los), tuple(his), tuple(fins), jnp.float32(1.0)))
    lo = jnp.concatenate(los, axis=0)
    surplus_ties = jnp.max(jnp.where(jnp.concatenate(fins, axis=0) == 1.0, 0.0, 1.0)) > 0.0
    kk = jnp.concatenate(kks, axis=0)

    def count_gt(thr):
        return jnp.sum(jnp.where(score_ref[...] > thr, 1.0, 0.0), axis=-1, keepdims=True)

    def snap_from(lo_v, strict):
        sc = score_ref[...]
        m = (sc > lo_v) if strict else (sc >= lo_v)
        return jnp.min(jnp.where(m, sc, pos_inf), axis=-1, keepdims=True)

    thr0 = snap_from(lo, False)
    above0 = count_gt(thr0)

    def refine_cond(carry):
        _, above = carry
        return jnp.max(jnp.where(above >= kk, 1.0, 0.0)) > 0.0

    def refine_body(carry):
        thr, above = carry
        thr = jnp.where(above >= kk, snap_from(thr, True), thr)
        return thr, count_gt(thr)

    thr, above = lax.while_loop(refine_cond, refine_body, (thr0, above0))

    @pl.when(jnp.logical_not(surplus_ties))
    def _():
        bias_ref[...] = jnp.where(score_ref[...] >= thr, 0.0, neg_inf)

    @pl.when(surplus_ties)
    def _():
        need = kk - above
        ones_blk = jnp.ones((LANES, LANES), BF16)
        offs = jnp.zeros((tq, LANES), F32)
        for c in range(width // LANES):
            sl = slice(c * LANES, (c + 1) * LANES)
            sc = score_ref[:, sl]
            tie = sc == thr
            tie_b = jnp.where(tie, 1.0, 0.0).astype(BF16)
            before = _dot(tie_b, tri_ref[...]) + offs
            sel = (sc > thr) | (tie & (before < need))
            bias_ref[:, sl] = jnp.where(sel, 0.0, neg_inf)
            offs = offs + _dot(tie_b, ones_blk)

    krot = kv_ref[:, 0:HEAD_DIM]
    vext = vext_ref[...]
    for h0 in range(0, ATTN_HEADS, DSA_HEAD_GROUP):
        heads = range(h0, h0 + DSA_HEAD_GROUP)
        lgs = [_dot_nt(q_ref[:, h * HEAD_DIM:(h + 1) * HEAD_DIM], krot) + bias_ref[...]
               for h in heads]
        mxs = [jnp.max(lg, axis=-1, keepdims=True) for lg in lgs]
        ps = [jnp.exp2(lg - mx).astype(BF16) for lg, mx in zip(lgs, mxs)]
        ovs = [_dot(p, vext) for p in ps]
        for h, ov in zip(heads, ovs):
            oh = ov[:, 0:HEAD_DIM] / ov[:, HEAD_DIM:HEAD_DIM + 1]
            o_ref[:, h * HEAD_DIM:(h + 1) * HEAD_DIM] = oh.astype(o_ref.dtype)


def _dsa(proj3, tri):
    batch, seq, _ = proj3.shape
    topk = min(TOPK_MAX, seq // 4)
    class_len = seq // DSA_CLASSES
    tq = min(DSA_TQ, class_len)
    tiles = class_len // tq
    outs = []
    for c in range(DSA_CLASSES):
        width = (c + 1) * class_len
        base = c * tiles
        kern = functools.partial(_dsa_kernel, tq=tq, width=width, topk=topk,
                                 row0=c * class_len)
        outs.append(pl.pallas_call(
            kern,
            grid=(batch, tiles),
            in_specs=[
                pl.BlockSpec((None, tq, ATTN_WIDTH),
                             lambda b, i, base=base: (b, base + i, COL_Q // ATTN_WIDTH)),
                pl.BlockSpec((None, tq, IDX_HEADS * IDX_DIM),
                             lambda b, i, base=base: (b, base + i, COL_QI // (IDX_HEADS * IDX_DIM))),
                pl.BlockSpec((None, width, LANES), lambda b, i: (b, 0, COL_KV // LANES)),
                pl.BlockSpec((None, width, LANES), lambda b, i: (b, 0, COL_KIW // LANES)),
                pl.BlockSpec((None, tq, LANES),
                             lambda b, i, base=base: (b, base + i, COL_KIW // LANES)),
                pl.BlockSpec((LANES, LANES), lambda b, i: (0, 0)),
            ],
            out_specs=pl.BlockSpec((None, tq, ATTN_WIDTH), lambda b, i: (b, i, 0)),
            out_shape=jax.ShapeDtypeStruct((batch, class_len, ATTN_WIDTH), BF16),
            scratch_shapes=[
                pltpu.VMEM((width, LANES), BF16),
                pltpu.VMEM((tq, width), F32),
                pltpu.VMEM((tq, width), F32),
            ],
            compiler_params=_cparams(("arbitrary", "arbitrary")),
            name=f"dsa_attention_w{width}",
        )(proj3, proj3, proj3, proj3, proj3, tri))
    return jnp.concatenate(outs, axis=1)


MERGE_ROWS = 512


def _top2_gates(logits):
    lane = lax.broadcasted_iota(jnp.int32, logits.shape, 1)
    neg_inf = jnp.float32(-jnp.inf)
    lg = jnp.where(lane < N_EXPERTS, logits, neg_inf)
    v1 = jnp.max(lg, axis=-1, keepdims=True)
    i1 = jnp.min(jnp.where(lg == v1, lane, LANES), axis=-1, keepdims=True)
    rest = jnp.where(lane == i1, neg_inf, lg)
    v2 = jnp.max(rest, axis=-1, keepdims=True)
    i2 = jnp.min(jnp.where(rest == v2, lane, LANES), axis=-1, keepdims=True)
    e2 = jnp.exp(v2 - v1)
    den = 1.0 + e2
    return jnp.where(lane == i1, 1.0 / den, 0.0) + jnp.where(lane == i2, e2 / den, 0.0)


def _merge_kernel(*refs, with_router):
    if with_router:
        (at_ref, mab_ref, g2_ref, x_ref, woc_ref, wo_ref, ln_ref, rw_ref,
         x1_ref, h2_ref, gate_ref) = refs
    else:
        at_ref, mab_ref, g2_ref, x_ref, woc_ref, wo_ref, ln_ref, x1_ref, h2_ref = refs
    y_c = _dot(at_ref[...], woc_ref[...])
    mixed = mab_ref[...].astype(F32) + _sigmoid(g2_ref[...].astype(F32)) * y_c
    x1 = x_ref[...] + _dot(mixed.astype(BF16), wo_ref[...])
    x1_ref[...] = x1
    ms = jnp.mean(x1 * x1, axis=-1, keepdims=True)
    h2 = ((x1 * lax.rsqrt(ms + EPS)) * ln_ref[...]).astype(BF16)
    h2_ref[...] = h2
    if with_router:
        gate_ref[...] = _top2_gates(_dot(h2, rw_ref[...]))


def _merge(attn, mab, proj, x2d, w_out_c, w_o, ln2_g, router_w):
    m = x2d.shape[0]
    with_router = router_w is not None
    row = lambda i: (i, 0)
    const = lambda i: (0, 0)
    in_specs = [
        pl.BlockSpec((MERGE_ROWS, ATTN_WIDTH), row),
        pl.BlockSpec((MERGE_ROWS, D_MODEL), row),
        pl.BlockSpec((MERGE_ROWS, D_MODEL), lambda i: (i, COL_GATES // D_MODEL + 2)),
        pl.BlockSpec((MERGE_ROWS, D_MODEL), row),
        pl.BlockSpec((ATTN_WIDTH, D_MODEL), const),
        pl.BlockSpec((D_MODEL, D_MODEL), const),
        pl.BlockSpec((1, D_MODEL), const),
    ]
    out_specs = [pl.BlockSpec((MERGE_ROWS, D_MODEL), row), pl.BlockSpec((MERGE_ROWS, D_MODEL), row)]
    out_shape = [jax.ShapeDtypeStruct((m, D_MODEL), F32), jax.ShapeDtypeStruct((m, D_MODEL), BF16)]
    args = [attn, mab, proj, x2d, w_out_c, w_o, ln2_g]
    if with_router:
        in_specs.append(pl.BlockSpec((D_MODEL, LANES), const))
        out_specs.append(pl.BlockSpec((MERGE_ROWS, LANES), row))
        out_shape.append(jax.ShapeDtypeStruct((m, LANES), F32))
        args.append(router_w)
    return pl.pallas_call(
        functools.partial(_merge_kernel, with_router=with_router),
        grid=(m // MERGE_ROWS,),
        in_specs=in_specs,
        out_specs=out_specs,
        out_shape=out_shape,
        compiler_params=_cparams(("parallel",)),
        name="merge_out_router" if with_router else "merge_out",
    )(*args)


FFN_ROWS = 512
FFN_TILE = 1408


def _swiglu_part(h, w1, w3, w2):
    a = _dot(h, w1)
    b = _dot(h, w3)
    return _dot(((a * _sigmoid(a)) * b).astype(BF16), w2)


def _ffn_kernel(h_ref, x_ref, w1_ref, w3_ref, w2_ref, o_ref):
    @pl.when(pl.program_id(1) == 0)
    def _():
        o_ref[...] = x_ref[...]

    o_ref[...] += _swiglu_part(h_ref[...], w1_ref[...], w3_ref[...], w2_ref[...])


def _ffn(h2, x1, w1, w3, w2):
    m = x1.shape[0]
    d_ff = w1.shape[1]
    row = lambda i, f: (i, 0)
    return pl.pallas_call(
        _ffn_kernel,
        grid=(m // FFN_ROWS, d_ff // FFN_TILE),
        in_specs=[
            pl.BlockSpec((FFN_ROWS, D_MODEL), row),
            pl.BlockSpec((FFN_ROWS, D_MODEL), row),
            pl.BlockSpec((D_MODEL, FFN_TILE), lambda i, f: (0, f)),
            pl.BlockSpec((D_MODEL, FFN_TILE), lambda i, f: (0, f)),
            pl.BlockSpec((FFN_TILE, D_MODEL), lambda i, f: (f, 0)),
        ],
        out_specs=pl.BlockSpec((FFN_ROWS, D_MODEL), row),
        out_shape=jax.ShapeDtypeStruct((m, D_MODEL), F32),
        compiler_params=_cparams(("parallel", "arbitrary")),
        name="dense_ffn",
    )(h2, x1, w1, w3, w2)


MOE_TILE = 512
MOE_FF_TILE = 1792
MOE_BLK = 256
MOE_SRC_BLK = 1024


def _count_le(sorted_ends, v):
    return jnp.sum((sorted_ends[None, :] <= v[:, None]).astype(jnp.int32), axis=1)


def _moe_route(gate):
    m = gate.shape[0]
    i32 = jnp.int32
    sel = gate[:, :N_EXPERTS] > 0.0
    seli = sel.astype(i32)
    csum = jnp.cumsum(seli, axis=0)
    rank = csum - seli
    cnt = csum[-1]
    gsz = (cnt + MOE_TILE - 1) // MOE_TILE * MOE_TILE
    gend = jnp.cumsum(gsz)
    goff = gend - gsz
    n_pos = 2 * m + N_EXPERTS * MOE_TILE
    pos = jnp.where(sel, goff[None, :] + rank, -1).astype(i32)

    n_ft = n_pos // MOE_TILE
    t0 = jnp.arange(n_ft, dtype=i32) * MOE_TILE
    tile_expert = jnp.minimum(_count_le(gend, t0), N_EXPERTS - 1)
    tile_active = (t0 < gend[-1]).astype(i32)

    n_db = n_pos // MOE_BLK
    n_sb = m // MOE_SRC_BLK
    d0 = jnp.arange(n_db, dtype=i32) * MOE_BLK
    e_d = jnp.minimum(_count_le(gend, d0), N_EXPERTS - 1)
    r0 = d0 - goff[e_d]
    cnt_d = cnt[e_d]
    has = (d0 < gend[-1]) & (r0 < cnt_d)
    r_last = jnp.minimum(r0 + MOE_BLK, cnt_d) - 1
    cb = csum[MOE_SRC_BLK - 1::MOE_SRC_BLK].T[e_d]
    b_lo = jnp.sum((cb <= r0[:, None]).astype(i32), axis=1)
    b_hi = jnp.sum((cb <= r_last[:, None]).astype(i32), axis=1)
    b_lo = jnp.where(has, b_lo, 0)
    n_it = jnp.where(has, b_hi - b_lo + 1, 1)
    it_end = jnp.cumsum(n_it)
    it_start = it_end - n_it
    n_items = n_db + N_EXPERTS * n_sb
    n = jnp.arange(n_items, dtype=i32)
    live = n < it_end[-1]
    d = jnp.minimum(_count_le(it_end, n), n_db - 1)
    item_tile = jnp.where(live, d, n_db).astype(i32)
    item_blk = jnp.where(live, b_lo[d] + n - it_start[d], 0).astype(i32)
    item_first = jnp.where(live, n == it_start[d], True).astype(i32)
    item_expert = jnp.where(live, e_d[d], 0).astype(i32)

    before = jnp.concatenate([jnp.zeros((1, N_EXPERTS), i32), csum], axis=0)[0:m:MOE_BLK]
    blk_a = ((goff[None, :] + before) // MOE_BLK).astype(i32).reshape(-1)
    return dict(pos=pos, pos_t=pos.T, tile_expert=tile_expert, tile_active=tile_active,
                item_tile=item_tile, item_blk=item_blk, item_first=item_first,
                item_expert=item_expert, blk_a=blk_a, n_pos=n_pos)


def _moe_gather_kernel(it_ref, ib_ref, if_ref, ie_ref, post_ref, h_ref, o_ref):
    n = pl.program_id(0)
    rel = post_ref[pl.ds(ie_ref[n], 1), :] - it_ref[n] * MOE_BLK
    row = lax.broadcasted_iota(jnp.int32, (MOE_BLK, MOE_SRC_BLK), 0)
    onehot = jnp.where(rel == row, 1.0, 0.0).astype(BF16)
    res = _dot(onehot, h_ref[...]).astype(o_ref.dtype)

    @pl.when(if_ref[n] == 1)
    def _():
        o_ref[...] = res

    @pl.when(if_ref[n] == 0)
    def _():
        o_ref[...] += res


def _moe_gather(route, h2):
    n_rows = route["n_pos"] + MOE_BLK
    n_items = route["item_tile"].shape[0]
    grid_spec = pltpu.PrefetchScalarGridSpec(
        num_scalar_prefetch=4,
        grid=(n_items,),
        in_specs=[
            pl.BlockSpec((N_EXPERTS, MOE_SRC_BLK), lambda n, it, ib, fi, ie: (0, ib[n])),
            pl.BlockSpec((MOE_SRC_BLK, D_MODEL), lambda n, it, ib, fi, ie: (ib[n], 0)),
        ],
        out_specs=pl.BlockSpec((MOE_BLK, D_MODEL), lambda n, it, ib, fi, ie: (it[n], 0)),
    )
    return pl.pallas_call(
        _moe_gather_kernel,
        grid_spec=grid_spec,
        out_shape=jax.ShapeDtypeStruct((n_rows, D_MODEL), BF16),
        compiler_params=_cparams(("arbitrary",)),
        name="moe_gather",
    )(route["item_tile"], route["item_blk"], route["item_first"], route["item_expert"],
      route["pos_t"], h2)


def _moe_ffn_kernel(te_ref, ta_ref, x_ref, w1_ref, w3_ref, w2_ref, o_ref, acc_ref):
    k = pl.program_id(0)
    f = pl.program_id(1)
    nf = pl.num_programs(1)
    active = ta_ref[k] == 1

    @pl.when(active)
    def _():
        part = _swiglu_part(x_ref[...], w1_ref[...], w3_ref[...], w2_ref[...].astype(BF16))

        @pl.when(f == 0)
        def _():
            acc_ref[...] = part

        @pl.when(f > 0)
        def _():
            acc_ref[...] += part

        @pl.when(f == nf - 1)
        def _():
            o_ref[...] = acc_ref[...].astype(o_ref.dtype)

    @pl.when(jnp.logical_not(active) & (f == nf - 1))
    def _():
        o_ref[...] = jnp.zeros_like(o_ref)


def _moe_ffn(route, xg, w1, w3, w2):
    n_pos = route["n_pos"]
    d_ff = w1.shape[2]
    grid_spec = pltpu.PrefetchScalarGridSpec(
        num_scalar_prefetch=2,
        grid=(n_pos // MOE_TILE, d_ff // MOE_FF_TILE),
        in_specs=[
            pl.BlockSpec((MOE_TILE, D_MODEL), lambda k, f, te, ta: (k, 0)),
            pl.BlockSpec((None, D_MODEL, MOE_FF_TILE), lambda k, f, te, ta: (te[k], 0, f)),
            pl.BlockSpec((None, D_MODEL, MOE_FF_TILE), lambda k, f, te, ta: (te[k], 0, f)),
            pl.BlockSpec((None, MOE_FF_TILE, D_MODEL), lambda k, f, te, ta: (te[k], f, 0)),
        ],
        out_specs=pl.BlockSpec((MOE_TILE, D_MODEL), lambda k, f, te, ta: (k, 0)),
        scratch_shapes=[pltpu.VMEM((MOE_TILE, D_MODEL), F32)],
    )
    return pl.pallas_call(
        _moe_ffn_kernel,
        grid_spec=grid_spec,
        out_shape=jax.ShapeDtypeStruct((n_pos, D_MODEL), BF16),
        compiler_params=_cparams(("arbitrary", "arbitrary")),
        name="moe_expert_ffn",
    )(route["tile_expert"], route["tile_active"], xg, w1, w3, w2)


def _moe_combine_kernel(ba_ref, x_ref, gate_ref, pos_ref, *refs):
    y_refs, o_ref = refs[:-1], refs[-1]
    tb = pl.program_id(0)
    lane = lax.broadcasted_iota(jnp.int32, (MOE_BLK, MOE_BLK), 1)
    out = x_ref[...]
    for e in range(N_EXPERTS):
        base = ba_ref[tb * N_EXPERTS + e] * MOE_BLK
        rel = pos_ref[:, e:e + 1] - base
        picked = None
        for half in range(2):
            onehot = jnp.where(rel - half * MOE_BLK == lane, 1.0, 0.0).astype(BF16)
            part = _dot(onehot, y_refs[2 * e + half][...])
            picked = part if picked is None else picked + part
        out = out + gate_ref[:, e:e + 1] * picked
    o_ref[...] = out


def _moe_combine(route, x1, gate, y):
    m = x1.shape[0]
    last_blk = route["n_pos"] // MOE_BLK - 1
    row = lambda tb, ba: (tb, 0)
    y_specs = []
    for e in range(N_EXPERTS):
        for half in range(2):
            y_specs.append(pl.BlockSpec(
                (MOE_BLK, D_MODEL),
                lambda tb, ba, e=e, half=half: (
                    jnp.minimum(ba[tb * N_EXPERTS + e] + half, last_blk), 0)))
    grid_spec = pltpu.PrefetchScalarGridSpec(
        num_scalar_prefetch=1,
        grid=(m // MOE_BLK,),
        in_specs=[
            pl.BlockSpec((MOE_BLK, D_MODEL), row),
            pl.BlockSpec((MOE_BLK, LANES), row),
            pl.BlockSpec((MOE_BLK, N_EXPERTS), row),
        ] + y_specs,
        out_specs=pl.BlockSpec((MOE_BLK, D_MODEL), row),
    )
    return pl.pallas_call(
        _moe_combine_kernel,
        grid_spec=grid_spec,
        out_shape=jax.ShapeDtypeStruct((m, D_MODEL), F32),
        compiler_params=_cparams(("arbitrary",)),
        name="moe_combine",
    )(route["blk_a"], x1, gate, route["pos"], *([y] * (2 * N_EXPERTS)))


def _moe(h2, x1, gate, w1, w3, w2):
    route = _moe_route(gate)
    xg = _moe_gather(route, h2)
    y = _moe_ffn(route, xg, w1, w3, w2)
    return _moe_combine(route, x1, gate, y)


WPREP_ROWS = 256
IN_PROJ_PIECES = ((0, 0, 2560), (COL_QI, 2688, 256), (COL_KV, 2560, 128), (COL_KIW, 2944, 128),
                  (COL_GATES, 3012, 3072))


def _wprep_kernel(w_ref, o_ref):
    for dst, src, width in IN_PROJ_PIECES:
        o_ref[:, dst:dst + width] = w_ref[:, src:src + width].astype(o_ref.dtype)


def _pad_in_proj(w_in):
    depth, d, in_cols = w_in.shape
    return pl.pallas_call(
        _wprep_kernel,
        grid=(depth, d // WPREP_ROWS),
        in_specs=[pl.BlockSpec((None, WPREP_ROWS, in_cols), lambda l, i: (l, i, 0))],
        out_specs=pl.BlockSpec((None, WPREP_ROWS, PROJ_COLS), lambda l, i: (l, i, 0)),
        out_shape=jax.ShapeDtypeStruct((depth, d, PROJ_COLS), BF16),
        compiler_params=_cparams(("parallel", "parallel")),
        name="in_proj_weight_layout",
    )(w_in)


def _s5_params(lam_re, lam_im, log_dt, b_re, b_im, c_re, c_im):
    lr = jnp.minimum(lam_re.astype(F32), -1e-4)
    li = lam_im.astype(F32)
    dt = jnp.exp(log_dt.astype(F32))[:, None]
    mag = jnp.exp(lr * dt)
    ang = li * dt
    ab_re = mag * jnp.cos(ang)
    ab_im = mag * jnp.sin(ang)
    nr = ab_re - 1.0
    ni = ab_im
    den = lr * lr + li * li
    coef_re = (nr * lr + ni * li) / den
    coef_im = (ni * lr - nr * li) / den
    bf_re = coef_re[:, :, None] * b_re - coef_im[:, :, None] * b_im
    bf_im = coef_re[:, :, None] * b_im + coef_im[:, :, None] * b_re
    gh = SSM_GROUPS // SSM_HALVES
    eye = jnp.eye(gh, dtype=F32)

    def blockdiag_b(w):
        w = w.reshape(SSM_HALVES, gh, SSM_STATE, SSM_GROUP)
        return jnp.einsum('jgph,gk->jghkp', w, eye).reshape(SSM_HALVES, SSM_HALF_CH, SSM_HALF_STATES)

    def blockdiag_c(w):
        w = w.reshape(SSM_HALVES, gh, SSM_GROUP, SSM_STATE)
        return jnp.einsum('jghp,gk->jgpkh', w, eye).reshape(SSM_HALVES, SSM_HALF_STATES, SSM_HALF_CH)

    bc = jnp.concatenate([blockdiag_b(bf_re), blockdiag_b(bf_im)], axis=2).astype(BF16)
    cc = jnp.concatenate([blockdiag_c(c_re.astype(F32)), -blockdiag_c(c_im.astype(F32))],
                         axis=1).astype(BF16)
    a_re8 = jnp.broadcast_to(ab_re.reshape(1, -1), (SUBLANES, SSM_GROUPS * SSM_STATE))
    a_im8 = jnp.broadcast_to(ab_im.reshape(1, -1), (SUBLANES, SSM_GROUPS * SSM_STATE))
    return bc, a_re8, a_im8, cc


def _rope_tables(seq):
    pos = jnp.arange(seq, dtype=F32)
    inv = ROPE_THETA ** (-jnp.arange(HEAD_DIM // 2, dtype=F32) / (HEAD_DIM // 2))
    ang = pos[:, None] * inv[None, :]
    cos, sin = jnp.cos(ang), jnp.sin(ang)
    cos2 = jnp.concatenate([cos, cos], axis=-1)
    sin2 = jnp.concatenate([-sin, sin], axis=-1)
    one, zero = jnp.ones_like(cos2), jnp.zeros_like(sin2)
    tab_a = jnp.concatenate([cos2, cos2, sin2, sin2], axis=-1)
    tab_b = jnp.concatenate([cos2, one, sin2, zero], axis=-1)
    return tab_a, tab_b


def kernel(x, ln1_g, w_in, conv_w, ssm_lam_re, ssm_lam_im, ssm_log_dt, ssm_b_re, ssm_b_im,
           ssm_c_re, ssm_c_im, ssm_d, ssm_w_glu, q_norm_g, k_norm_g, w_out_a, w_out_b, w_out_c,
           w_o, ln2_g, ffn_w1, ffn_w3, ffn_w2, router_w, moe_w1, moe_w3, moe_w2):
    batch, seq, d = x.shape
    assert batch == SUBLANES and d == D_MODEL
    depth = w_in.shape[0]
    m = batch * seq
    tab_a, tab_b = _rope_tables(seq)
    ii = lax.broadcasted_iota(jnp.int32, (LANES, LANES), 0)
    jj = lax.broadcasted_iota(jnp.int32, (LANES, LANES), 1)
    tri = (ii < jj).astype(BF16)
    gsum = ((ii // HEAD_DIM) == (jj // HEAD_DIM)).astype(BF16)
    r_tm = lax.broadcasted_iota(jnp.int32, (MIX_ROWS, MIX_ROWS), 0)
    r_bm = lax.broadcasted_iota(jnp.int32, (MIX_ROWS, MIX_ROWS), 1)
    perm = ((r_tm % SUBLANES) * MIX_STEPS + r_tm // SUBLANES == r_bm).astype(BF16)
    perm_t = perm.T

    w_in_p = _pad_in_proj(w_in)
    xt = x.reshape(m, d)
    for layer in range(depth):
        q_gain2 = jnp.tile(q_norm_g[layer], 2)[None, :]
        k_gain2 = jnp.tile(k_norm_g[layer], 2)[None, :]
        proj = _inproj(xt, ln1_g[layer][None, :], w_in_p, layer, tab_a, tab_b,
                       q_gain2, k_gain2, gsum, seq)
        proj3 = proj.reshape(batch, seq, PROJ_COLS)
        bc, a_re8, a_im8, cc = _s5_params(
            ssm_lam_re[layer], ssm_lam_im[layer], ssm_log_dt[layer], ssm_b_re[layer],
            ssm_b_im[layer], ssm_c_re[layer], ssm_c_im[layer])
        mab = _mixer(proj3, conv_w[layer], perm, perm_t, bc, a_re8, a_im8, cc,
                     ssm_d[layer][None, :], ssm_w_glu[layer].astype(BF16),
                     w_out_a[layer].astype(BF16), w_out_b[layer].astype(BF16))
        mab = mab.reshape(m, D_MODEL)
        attn = _dsa(proj3, tri).reshape(m, ATTN_WIDTH)
        j = layer // 2
        if layer % 2 == 0:
            x1, h2 = _merge(attn, mab, proj, xt, w_out_c[layer].astype(BF16),
                            w_o[layer].astype(BF16), ln2_g[layer][None, :], None)
            xt = _ffn(h2, x1, ffn_w1[j].astype(BF16), ffn_w3[j].astype(BF16),
                      ffn_w2[j].astype(BF16))
        else:
            rw = jnp.pad(router_w[j], ((0, 0), (0, LANES - N_EXPERTS))).astype(BF16)
            x1, h2, gate = _merge(attn, mab, proj, xt, w_out_c[layer].astype(BF16),
                                  w_o[layer].astype(BF16), ln2_g[layer][None, :], rw)
            xt = _moe(h2, x1, gate, moe_w1[j].astype(BF16), moe_w3[j].astype(BF16), moe_w2[j])
    return xt.reshape(batch, seq, d)
```

```python
import functools

import jax
import jax.numpy as jnp
from jax import lax
from jax.experimental import pallas as pl
from jax.experimental.pallas import tpu as pltpu

F32 = jnp.float32
BF16 = jnp.bfloat16

D_MODEL = 1024
CONV_WIDTH = 512
CONV_K = 3
SSM_WIDTH = 512
SSM_GROUP = 16
SSM_GROUPS = 32
SSM_STATE = 64
ATTN_HEADS = 8
HEAD_DIM = 64
ATTN_WIDTH = 512
IDX_HEADS = 4
IDX_DIM = 64
TOPK_MAX = 256
ROPE_THETA = 10000.0
N_EXPERTS = 8
EPS = 1e-6
LOG2_E = 1.4426950408889634

SUBLANES = 8
LANES = 128
VMEM_LIMIT_BYTES = 56 * 1024 * 1024

COL_CONV = 0
COL_SSM = 1536
COL_Q = 2048
COL_QI = 2560
COL_KV = 2816
COL_KIW = 2944
COL_GATES = 3072
PROJ_COLS = 6144

SSM_HALVES = 2
SSM_HALF_STATES = SSM_GROUPS // SSM_HALVES * SSM_STATE
SSM_HALF_CH = SSM_WIDTH // SSM_HALVES
SSM_COLS = 2 * SSM_GROUPS * SSM_STATE
SCAN_COLS = 512

BISECT_ITERS = 20
BISECT_UNROLL = 4


def _cparams(sem):
    return pltpu.CompilerParams(dimension_semantics=sem, vmem_limit_bytes=VMEM_LIMIT_BYTES)


def _sigmoid(x):
    return 1.0 / (1.0 + jnp.exp(-x))


def _dot(a, b):
    return jnp.dot(a, b, preferred_element_type=F32)


def _dot_nt(a, b):
    return lax.dot_general(a, b, (((1,), (1,)), ((), ())), preferred_element_type=F32)


INPROJ_ROWS = 512
INPROJ_NCHUNK = 512


def _rope_block(x, cos, sin):
    lane = lax.broadcasted_iota(jnp.int32, x.shape, 1)
    first_half = (lane % HEAD_DIM) < (HEAD_DIM // 2)
    swapped = jnp.where(first_half, pltpu.roll(x, LANES - HEAD_DIM // 2, 1),
                        pltpu.roll(x, HEAD_DIM // 2, 1))
    return x * cos + swapped * sin


def _head_inv_rms(x, gsum):
    sq = x * x
    hi = sq.astype(BF16)
    lo = (sq - hi.astype(F32)).astype(BF16)
    ss = _dot(hi, gsum) + _dot(lo, gsum)
    return lax.rsqrt(ss * (1.0 / HEAD_DIM) + EPS)


def _inproj_kernel(x_ref, g_ref, w_ref, taba_ref, tabb_ref, qg_ref, kg_ref, gsum_ref, o_ref):
    x = x_ref[...]
    ms = jnp.mean(x * x, axis=-1, keepdims=True)
    h = ((x * lax.rsqrt(ms + EPS)) * g_ref[...]).astype(BF16)
    gsum = gsum_ref[...]
    cos_a, sin_a = taba_ref[:, 0:LANES], taba_ref[:, LANES:2 * LANES]
    cos_b, sin_b = tabb_ref[:, 0:LANES], tabb_ref[:, LANES:2 * LANES]
    for n in range(PROJ_COLS // INPROJ_NCHUNK):
        c0 = n * INPROJ_NCHUNK
        res = _dot(h, w_ref[:, c0:c0 + INPROJ_NCHUNK])
        if c0 == COL_Q:
            for blk in range(INPROJ_NCHUNK // LANES):
                xb = res[:, blk * LANES:(blk + 1) * LANES]
                xb = (xb * _head_inv_rms(xb, gsum)) * qg_ref[...]
                xb = _rope_block(xb, cos_a, sin_a) * (HEAD_DIM ** -0.5 * LOG2_E)
                o_ref[:, c0 + blk * LANES:c0 + (blk + 1) * LANES] = xb.astype(o_ref.dtype)
        elif c0 == COL_QI:
            for blk in range(INPROJ_NCHUNK // LANES):
                xb = res[:, blk * LANES:(blk + 1) * LANES]
                col = c0 + blk * LANES
                if col < COL_KV:
                    xb = _rope_block(xb, cos_a, sin_a)
                elif col == COL_KV:
                    lane = lax.broadcasted_iota(jnp.int32, xb.shape, 1)
                    xn = (xb * _head_inv_rms(xb, gsum)) * kg_ref[...]
                    xb = _rope_block(jnp.where(lane < HEAD_DIM, xn, xb), cos_b, sin_b)
                else:
                    xb = _rope_block(xb, cos_b, sin_b)
                o_ref[:, col:col + LANES] = xb.astype(o_ref.dtype)
        else:
            o_ref[:, c0:c0 + INPROJ_NCHUNK] = res.astype(o_ref.dtype)


def _inproj(x2d, ln_g, w_in_p, layer, tab_a, tab_b, q_gain2, k_gain2, gsum, seq):
    m = x2d.shape[0]
    tiles_per_seq = seq // INPROJ_ROWS
    const = lambda i: (0, 0)
    return pl.pallas_call(
        _inproj_kernel,
        grid=(m // INPROJ_ROWS,),
        in_specs=[
            pl.BlockSpec((INPROJ_ROWS, D_MODEL), lambda i: (i, 0)),
            pl.BlockSpec((1, D_MODEL), const),
            pl.BlockSpec((None, D_MODEL, PROJ_COLS), lambda i: (layer, 0, 0)),
            pl.BlockSpec((INPROJ_ROWS, 2 * LANES), lambda i: (i % tiles_per_seq, 0)),
            pl.BlockSpec((INPROJ_ROWS, 2 * LANES), lambda i: (i % tiles_per_seq, 0)),
            pl.BlockSpec((1, LANES), const),
            pl.BlockSpec((1, LANES), const),
            pl.BlockSpec((LANES, LANES), const),
        ],
        out_specs=pl.BlockSpec((INPROJ_ROWS, PROJ_COLS), lambda i: (i, 0)),
        out_shape=jax.ShapeDtypeStruct((m, PROJ_COLS), BF16),
        compiler_params=_cparams(("parallel",)),
        name="inproj",
    )(x2d, ln_g, w_in_p, tab_a, tab_b, q_gain2, k_gain2, gsum)


MIX_STEPS = 64
MIX_ROWS = MIX_STEPS * SUBLANES


def _mixer_kernel(pa_ref, su_ref, g0_ref, g1_ref, cw_ref, perm_ref, permt_ref, bc_ref, are_ref,
                  aim_ref, cc_ref, d_ref, wglu_ref, woa_ref, wob_ref, o_ref,
                  xs_ref, hst_ref, vext_ref):
    i = pl.program_id(0)
    rows = MIX_ROWS
    tc = MIX_STEPS

    @pl.when(i == 0)
    def _():
        hst_ref[...] = jnp.zeros_like(hst_ref)
        vext_ref[:, 0:SUBLANES, :] = jnp.zeros((SUBLANES, SUBLANES, CONV_WIDTH), F32)

    u = pa_ref[:, :, 0:CONV_WIDTH].astype(F32)
    bg = pa_ref[:, :, CONV_WIDTH:2 * CONV_WIDTH].astype(F32)
    cg = pa_ref[:, :, 2 * CONV_WIDTH:3 * CONV_WIDTH].astype(F32)
    v = cg * u
    vext_ref[:, SUBLANES:SUBLANES + tc, :] = v
    y = (cw_ref[0:1, :] * vext_ref[:, SUBLANES - 2:SUBLANES - 2 + tc, :]
         + cw_ref[1:2, :] * vext_ref[:, SUBLANES - 1:SUBLANES - 1 + tc, :]
         + cw_ref[2:3, :] * v)
    vext_ref[:, 0:SUBLANES, :] = v[:, tc - SUBLANES:tc, :]
    y_a = _dot((bg * y).reshape(rows, CONV_WIDTH).astype(BF16), woa_ref[...])

    u_tm = _dot(perm_ref[...], su_ref[...].reshape(rows, SSM_WIDTH))
    u_tm_b = u_tm.astype(BF16)
    for j in range(SSM_HALVES):
        xs_ref[:, j * 2 * SSM_HALF_STATES:(j + 1) * 2 * SSM_HALF_STATES] = _dot(
            u_tm_b[:, j * SSM_HALF_CH:(j + 1) * SSM_HALF_CH], bc_ref[j])

    for j in range(SSM_HALVES):
        for q in range(SSM_HALF_STATES // SCAN_COLS):
            cr = j * 2 * SSM_HALF_STATES + q * SCAN_COLS
            ci = cr + SSM_HALF_STATES
            ca = j * SSM_HALF_STATES + q * SCAN_COLS
            ar = are_ref[:, ca:ca + SCAN_COLS]
            ai = aim_ref[:, ca:ca + SCAN_COLS]

            def step(t, carry, cr=cr, ci=ci, ar=ar, ai=ai):
                hr, hi = carry
                r0 = pl.multiple_of(t * SUBLANES, SUBLANES)
                xr = xs_ref[pl.ds(r0, SUBLANES), cr:cr + SCAN_COLS]
                xi = xs_ref[pl.ds(r0, SUBLANES), ci:ci + SCAN_COLS]
                nr = ar * hr - ai * hi + xr
                ni = ar * hi + ai * hr + xi
                xs_ref[pl.ds(r0, SUBLANES), cr:cr + SCAN_COLS] = nr
                xs_ref[pl.ds(r0, SUBLANES), ci:ci + SCAN_COLS] = ni
                return nr, ni

            hr, hi = lax.fori_loop(
                0, MIX_STEPS, step,
                (hst_ref[:, cr:cr + SCAN_COLS], hst_ref[:, ci:ci + SCAN_COLS]), unroll=True)
            hst_ref[:, cr:cr + SCAN_COLS] = hr
            hst_ref[:, ci:ci + SCAN_COLS] = hi

    ys = []
    for j in range(SSM_HALVES):
        hj = xs_ref[:, j * 2 * SSM_HALF_STATES:(j + 1) * 2 * SSM_HALF_STATES].astype(BF16)
        ys.append(_dot(hj, cc_ref[j]))
    ysum = jnp.concatenate(ys, axis=-1) + d_ref[...] * u_tm
    yg = jax.nn.gelu(ysum, approximate=True)
    yg = (yg * _sigmoid(_dot(yg.astype(BF16), wglu_ref[...]))).astype(BF16)
    yg_bm = _dot(permt_ref[...], yg).astype(BF16)
    y_b = _dot(yg_bm, wob_ref[...])

    g0 = g0_ref[...].reshape(rows, D_MODEL).astype(F32)
    g1 = g1_ref[...].reshape(rows, D_MODEL).astype(F32)
    out = _sigmoid(g0) * y_a + _sigmoid(g1) * y_b
    o_ref[...] = out.reshape(SUBLANES, tc, D_MODEL).astype(o_ref.dtype)


def _mixer(proj3, conv_w, perm, perm_t, bc, a_re8, a_im8, cc, d_skip, w_glu, w_out_a, w_out_b):
    batch, seq, _ = proj3.shape
    const2 = lambda i: (0, 0)
    const3 = lambda i: (0, 0, 0)
    return pl.pallas_call(
        _mixer_kernel,
        grid=(seq // MIX_STEPS,),
        in_specs=[
            pl.BlockSpec((batch, MIX_STEPS, 3 * CONV_WIDTH),
                         lambda i: (0, i, COL_CONV // (3 * CONV_WIDTH))),
            pl.BlockSpec((batch, MIX_STEPS, SSM_WIDTH), lambda i: (0, i, COL_SSM // SSM_WIDTH)),
            pl.BlockSpec((batch, MIX_STEPS, D_MODEL), lambda i: (0, i, COL_GATES // D_MODEL)),
            pl.BlockSpec((batch, MIX_STEPS, D_MODEL), lambda i: (0, i, COL_GATES // D_MODEL + 1)),
            pl.BlockSpec((CONV_K, CONV_WIDTH), const2),
            pl.BlockSpec((MIX_ROWS, MIX_ROWS), const2),
            pl.BlockSpec((MIX_ROWS, MIX_ROWS), const2),
            pl.BlockSpec((SSM_HALVES, SSM_HALF_CH, 2 * SSM_HALF_STATES), const3),
            pl.BlockSpec((SUBLANES, SSM_GROUPS * SSM_STATE), const2),
            pl.BlockSpec((SUBLANES, SSM_GROUPS * SSM_STATE), const2),
            pl.BlockSpec((SSM_HALVES, 2 * SSM_HALF_STATES, SSM_HALF_CH), const3),
            pl.BlockSpec((1, SSM_WIDTH), const2),
            pl.BlockSpec((SSM_WIDTH, SSM_WIDTH), const2),
            pl.BlockSpec((CONV_WIDTH, D_MODEL), const2),
            pl.BlockSpec((SSM_WIDTH, D_MODEL), const2),
        ],
        out_specs=pl.BlockSpec((batch, MIX_STEPS, D_MODEL), lambda i: (0, i, 0)),
        out_shape=jax.ShapeDtypeStruct((batch, seq, D_MODEL), BF16),
        scratch_shapes=[
            pltpu.VMEM((MIX_ROWS, SSM_COLS), F32),
            pltpu.VMEM((SUBLANES, SSM_COLS), F32),
            pltpu.VMEM((SUBLANES, MIX_STEPS + SUBLANES, CONV_WIDTH), F32),
        ],
        compiler_params=_cparams(("arbitrary",)),
        name="conv_s5_mixer",
    )(proj3, proj3, proj3, proj3, conv_w, perm, perm_t, bc, a_re8, a_im8, cc, d_skip, w_glu,
      w_out_a, w_out_b)


DSA_TQ = 512
DSA_CLASSES = 4
DSA_CHAINS = 2
DSA_HEAD_GROUP = 4


def _dsa_kernel(q_ref, qi_ref, kv_ref, kiw_ref, kiwq_ref, tri_ref, o_ref,
                vext_ref, score_ref, bias_ref, *, tq, width, topk, row0):
    i = pl.program_id(1)
    rows_c = tq // DSA_CHAINS
    neg_inf = jnp.float32(-jnp.inf)
    pos_inf = jnp.float32(jnp.inf)

    @pl.when(i == 0)
    def _():
        lane = lax.broadcasted_iota(jnp.int32, (width, LANES), 1)
        shifted = pltpu.roll(kv_ref[...].astype(F32), HEAD_DIM, 1)
        vext_ref[...] = jnp.where(lane < HEAD_DIM, shifted, 1.0).astype(BF16)

    t0 = row0 + i * tq

    wq = kiwq_ref[:, IDX_DIM:IDX_DIM + IDX_HEADS].astype(F32) * (IDX_HEADS ** -0.5)
    kirot = kiw_ref[:, 0:IDX_DIM]
    score = None
    for h in range(IDX_HEADS):
        rel = jnp.maximum(_dot_nt(qi_ref[:, h * IDX_DIM:(h + 1) * IDX_DIM], kirot), 0.0)
        rel = rel * wq[:, h:h + 1]
        score = rel if score is None else score + rel
    col = lax.broadcasted_iota(jnp.int32, (tq, width), 1)
    row = t0 + lax.broadcasted_iota(jnp.int32, (tq, width), 0)
    score_ref[...] = jnp.where(col <= row, score, neg_inf)

    def chain_rows(c):
        return slice(c * rows_c, (c + 1) * rows_c)

    kks, los, his, fins = [], [], [], []
    for c in range(DSA_CHAINS):
        sc = score_ref[chain_rows(c), :]
        t_row = t0 + c * rows_c + lax.broadcasted_iota(jnp.int32, (rows_c, 1), 0)
        kk_c = jnp.minimum(t_row + 1, topk).astype(F32)
        hi_c = jnp.max(sc, axis=-1, keepdims=True)
        lo_c = jnp.min(jnp.where(sc == neg_inf, pos_inf, sc), axis=-1, keepdims=True)
        n_ge0 = jnp.sum(jnp.where(sc >= 0.0, 1.0, 0.0), axis=-1, keepdims=True)
        n_gt0 = jnp.sum(jnp.where(sc > 0.0, 1.0, 0.0), axis=-1, keepdims=True)
        above0 = n_gt0 >= kk_c
        reach0 = n_ge0 >= kk_c
        take_all = t_row < topk
        kks.append(kk_c)
        los.append(jnp.where(take_all, lo_c, jnp.where(reach0, 0.0, lo_c)))
        his.append(jnp.where(take_all, lo_c, jnp.where(above0, hi_c, 0.0)))
        fins.append(jnp.where(take_all | (reach0 & jnp.logical_not(above0)), 1.0, 0.0))

    def bisect_round(carry):
        it, los_c, his_c, fins_c, _ = carry
        open_rows = jnp.float32(0.0)
        for _ in range(BISECT_UNROLL):
            new_lo, new_hi, new_fin = [], [], []
            open_rows = jnp.float32(0.0)
            for c in range(DSA_CHAINS):
                sc = score_ref[chain_rows(c), :]
                mid = 0.5 * los_c[c] + 0.5 * his_c[c]
                cnt = jnp.sum(jnp.where(sc >= mid, 1.0, 0.0), axis=-1, keepdims=True)
                live = fins_c[c] == 0.0
                ge = cnt >= kks[c]
                hit = live & (cnt == kks[c])
                new_lo.append(jnp.where(live & ge, mid, los_c[c]))
                new_hi.append(jnp.where(live & (hit | jnp.logical_not(ge)), mid, his_c[c]))
                fin = jnp.where(hit, 1.0, fins_c[c])
                new_fin.append(fin)
                open_rows = jnp.maximum(open_rows, jnp.max(1.0 - fin))
            los_c, his_c, fins_c = tuple(new_lo), tuple(new_hi), tuple(new_fin)
        return it + BISECT_UNROLL, los_c, his_c, fins_c, open_rows

    def bisect_cond(carry):
        return (carry[0] < BISECT_ITERS) & (carry[4] > 0.0)

    _, los, _, _, _ = lax.while_loop(
        bisect_cond, bisect_round,
        (jnp.int32(0), tuple(los), tuple(his), tuple(fins), jnp.float32(1.0)))
    lo = jnp.concatenate(los, axis=0)
    kk = jnp.concatenate(kks, axis=0)

    def count_gt(thr):
        return jnp.sum(jnp.where(score_ref[...] > thr, 1.0, 0.0), axis=-1, keepdims=True)

    def snap_from(lo_v, strict):
        sc = score_ref[...]
        m = (sc > lo_v) if strict else (sc >= lo_v)
        return jnp.min(jnp.where(m, sc, pos_inf), axis=-1, keepdims=True)

    thr0 = snap_from(lo, False)
    above0 = count_gt(thr0)

    def refine_cond(carry):
        _, above = carry
        return jnp.max(jnp.where(above >= kk, 1.0, 0.0)) > 0.0

    def refine_body(carry):
        thr, above = carry
        thr = jnp.where(above >= kk, snap_from(thr, True), thr)
        return thr, count_gt(thr)

    thr, above = lax.while_loop(refine_cond, refine_body, (thr0, above0))

    need = kk - above
    ones_blk = jnp.ones((LANES, LANES), BF16)
    offs = jnp.zeros((tq, LANES), F32)
    for c in range(width // LANES):
        sl = slice(c * LANES, (c + 1) * LANES)
        sc = score_ref[:, sl]
        tie = sc == thr
        tie_b = jnp.where(tie, 1.0, 0.0).astype(BF16)
        before = _dot(tie_b, tri_ref[...]) + offs
        sel = (sc > thr) | (tie & (before < need))
        bias_ref[:, sl] = jnp.where(sel, 0.0, neg_inf)
        offs = offs + _dot(tie_b, ones_blk)

    krot = kv_ref[:, 0:HEAD_DIM]
    vext = vext_ref[...]
    for h0 in range(0, ATTN_HEADS, DSA_HEAD_GROUP):
        heads = range(h0, h0 + DSA_HEAD_GROUP)
        lgs = [_dot_nt(q_ref[:, h * HEAD_DIM:(h + 1) * HEAD_DIM], krot) + bias_ref[...]
               for h in heads]
        mxs = [jnp.max(lg, axis=-1, keepdims=True) for lg in lgs]
        ps = [jnp.exp2(lg - mx).astype(BF16) for lg, mx in zip(lgs, mxs)]
        ovs = [_dot(p, vext) for p in ps]
        for h, ov in zip(heads, ovs):
            oh = ov[:, 0:HEAD_DIM] / ov[:, HEAD_DIM:HEAD_DIM + 1]
            o_ref[:, h * HEAD_DIM:(h + 1) * HEAD_DIM] = oh.astype(o_ref.dtype)


def _dsa(proj3, tri):
    batch, seq, _ = proj3.shape
    topk = min(TOPK_MAX, seq // 4)
    class_len = seq // DSA_CLASSES
    tq = min(DSA_TQ, class_len)
    tiles = class_len // tq
    outs = []
    for c in range(DSA_CLASSES):
        width = (c + 1) * class_len
        base = c * tiles
        kern = functools.partial(_dsa_kernel, tq=tq, width=width, topk=topk,
                                 row0=c * class_len)
        outs.append(pl.pallas_call(
            kern,
            grid=(batch, tiles),
            in_specs=[
                pl.BlockSpec((None, tq, ATTN_WIDTH),
                             lambda b, i, base=base: (b, base + i, COL_Q // ATTN_WIDTH)),
                pl.BlockSpec((None, tq, IDX_HEADS * IDX_DIM),
                             lambda b, i, base=base: (b, base + i, COL_QI // (IDX_HEADS * IDX_DIM))),
                pl.BlockSpec((None, width, LANES), lambda b, i: (b, 0, COL_KV // LANES)),
                pl.BlockSpec((None, width, LANES), lambda b, i: (b, 0, COL_KIW // LANES)),
                pl.BlockSpec((None, tq, LANES),
                             lambda b, i, base=base: (b, base + i, COL_KIW // LANES)),
                pl.BlockSpec((LANES, LANES), lambda b, i: (0, 0)),
            ],
            out_specs=pl.BlockSpec((None, tq, ATTN_WIDTH), lambda b, i: (b, i, 0)),
            out_shape=jax.ShapeDtypeStruct((batch, class_len, ATTN_WIDTH), BF16),
            scratch_shapes=[
                pltpu.VMEM((width, LANES), BF16),
                pltpu.VMEM((tq, width), F32),
                pltpu.VMEM((tq, width), F32),
            ],
            compiler_params=_cparams(("arbitrary", "arbitrary")),
            name=f"dsa_attention_w{width}",
        )(proj3, proj3, proj3, proj3, proj3, tri))
    return jnp.concatenate(outs, axis=1)


MERGE_ROWS = 512


def _top2_gates(logits):
    lane = lax.broadcasted_iota(jnp.int32, logits.shape, 1)
    neg_inf = jnp.float32(-jnp.inf)
    lg = jnp.where(lane < N_EXPERTS, logits, neg_inf)
    v1 = jnp.max(lg, axis=-1, keepdims=True)
    i1 = jnp.min(jnp.where(lg == v1, lane, LANES), axis=-1, keepdims=True)
    rest = jnp.where(lane == i1, neg_inf, lg)
    v2 = jnp.max(rest, axis=-1, keepdims=True)
    i2 = jnp.min(jnp.where(rest == v2, lane, LANES), axis=-1, keepdims=True)
    e2 = jnp.exp(v2 - v1)
    den = 1.0 + e2
    return jnp.where(lane == i1, 1.0 / den, 0.0) + jnp.where(lane == i2, e2 / den, 0.0)


def _merge_kernel(*refs, with_router):
    if with_router:
        (at_ref, mab_ref, g2_ref, x_ref, woc_ref, wo_ref, ln_ref, rw_ref,
         x1_ref, h2_ref, gate_ref) = refs
    else:
        at_ref, mab_ref, g2_ref, x_ref, woc_ref, wo_ref, ln_ref, x1_ref, h2_ref = refs
    y_c = _dot(at_ref[...], woc_ref[...])
    mixed = mab_ref[...].astype(F32) + _sigmoid(g2_ref[...].astype(F32)) * y_c
    x1 = x_ref[...] + _dot(mixed.astype(BF16), wo_ref[...])
    x1_ref[...] = x1
    ms = jnp.mean(x1 * x1, axis=-1, keepdims=True)
    h2 = ((x1 * lax.rsqrt(ms + EPS)) * ln_ref[...]).astype(BF16)
    h2_ref[...] = h2
    if with_router:
        gate_ref[...] = _top2_gates(_dot(h2, rw_ref[...]))


def _merge(attn, mab, proj, x2d, w_out_c, w_o, ln2_g, router_w):
    m = x2d.shape[0]
    with_router = router_w is not None
    row = lambda i: (i, 0)
    const = lambda i: (0, 0)
    in_specs = [
        pl.BlockSpec((MERGE_ROWS, ATTN_WIDTH), row),
        pl.BlockSpec((MERGE_ROWS, D_MODEL), row),
        pl.BlockSpec((MERGE_ROWS, D_MODEL), lambda i: (i, COL_GATES // D_MODEL + 2)),
        pl.BlockSpec((MERGE_ROWS, D_MODEL), row),
        pl.BlockSpec((ATTN_WIDTH, D_MODEL), const),
        pl.BlockSpec((D_MODEL, D_MODEL), const),
        pl.BlockSpec((1, D_MODEL), const),
    ]
    out_specs = [pl.BlockSpec((MERGE_ROWS, D_MODEL), row), pl.BlockSpec((MERGE_ROWS, D_MODEL), row)]
    out_shape = [jax.ShapeDtypeStruct((m, D_MODEL), F32), jax.ShapeDtypeStruct((m, D_MODEL), BF16)]
    args = [attn, mab, proj, x2d, w_out_c, w_o, ln2_g]
    if with_router:
        in_specs.append(pl.BlockSpec((D_MODEL, LANES), const))
        out_specs.append(pl.BlockSpec((MERGE_ROWS, LANES), row))
        out_shape.append(jax.ShapeDtypeStruct((m, LANES), F32))
        args.append(router_w)
    return pl.pallas_call(
        functools.partial(_merge_kernel, with_router=with_router),
        grid=(m // MERGE_ROWS,),
        in_specs=in_specs,
        out_specs=out_specs,
        out_shape=out_shape,
        compiler_params=_cparams(("parallel",)),
        name="merge_out_router" if with_router else "merge_out",
    )(*args)


FFN_ROWS = 512
FFN_TILE = 1408


def _swiglu_part(h, w1, w3, w2):
    a = _dot(h, w1)
    b = _dot(h, w3)
    return _dot(((a * _sigmoid(a)) * b).astype(BF16), w2)


def _ffn_kernel(h_ref, x_ref, w1_ref, w3_ref, w2_ref, o_ref):
    @pl.when(pl.program_id(1) == 0)
    def _():
        o_ref[...] = x_ref[...]

    o_ref[...] += _swiglu_part(h_ref[...], w1_ref[...], w3_ref[...], w2_ref[...])


def _ffn(h2, x1, w1, w3, w2):
    m = x1.shape[0]
    d_ff = w1.shape[1]
    row = lambda i, f: (i, 0)
    return pl.pallas_call(
        _ffn_kernel,
        grid=(m // FFN_ROWS, d_ff // FFN_TILE),
        in_specs=[
            pl.BlockSpec((FFN_ROWS, D_MODEL), row),
            pl.BlockSpec((FFN_ROWS, D_MODEL), row),
            pl.BlockSpec((D_MODEL, FFN_TILE), lambda i, f: (0, f)),
            pl.BlockSpec((D_MODEL, FFN_TILE), lambda i, f: (0, f)),
            pl.BlockSpec((FFN_TILE, D_MODEL), lambda i, f: (f, 0)),
        ],
        out_specs=pl.BlockSpec((FFN_ROWS, D_MODEL), row),
        out_shape=jax.ShapeDtypeStruct((m, D_MODEL), F32),
        compiler_params=_cparams(("parallel", "arbitrary")),
        name="dense_ffn",
    )(h2, x1, w1, w3, w2)


MOE_TILE = 512
MOE_FF_TILE = 1792
MOE_BLK = 256
MOE_SRC_BLK = 1024


def _count_le(sorted_ends, v):
    return jnp.sum((sorted_ends[None, :] <= v[:, None]).astype(jnp.int32), axis=1)


def _moe_route(gate):
    m = gate.shape[0]
    i32 = jnp.int32
    sel = gate[:, :N_EXPERTS] > 0.0
    seli = sel.astype(i32)
    csum = jnp.cumsum(seli, axis=0)
    rank = csum - seli
    cnt = csum[-1]
    gsz = (cnt + MOE_TILE - 1) // MOE_TILE * MOE_TILE
    gend = jnp.cumsum(gsz)
    goff = gend - gsz
    n_pos = 2 * m + N_EXPERTS * MOE_TILE
    pos = jnp.where(sel, goff[None, :] + rank, -1).astype(i32)

    n_ft = n_pos // MOE_TILE
    t0 = jnp.arange(n_ft, dtype=i32) * MOE_TILE
    tile_expert = jnp.minimum(_count_le(gend, t0), N_EXPERTS - 1)
    tile_active = (t0 < gend[-1]).astype(i32)

    n_db = n_pos // MOE_BLK
    n_sb = m // MOE_SRC_BLK
    d0 = jnp.arange(n_db, dtype=i32) * MOE_BLK
    e_d = jnp.minimum(_count_le(gend, d0), N_EXPERTS - 1)
    r0 = d0 - goff[e_d]
    cnt_d = cnt[e_d]
    has = (d0 < gend[-1]) & (r0 < cnt_d)
    r_last = jnp.minimum(r0 + MOE_BLK, cnt_d) - 1
    cb = csum[MOE_SRC_BLK - 1::MOE_SRC_BLK].T[e_d]
    b_lo = jnp.sum((cb <= r0[:, None]).astype(i32), axis=1)
    b_hi = jnp.sum((cb <= r_last[:, None]).astype(i32), axis=1)
    b_lo = jnp.where(has, b_lo, 0)
    n_it = jnp.where(has, b_hi - b_lo + 1, 1)
    it_end = jnp.cumsum(n_it)
    it_start = it_end - n_it
    n_items = n_db + N_EXPERTS * n_sb
    n = jnp.arange(n_items, dtype=i32)
    live = n < it_end[-1]
    d = jnp.minimum(_count_le(it_end, n), n_db - 1)
    item_tile = jnp.where(live, d, n_db).astype(i32)
    item_blk = jnp.where(live, b_lo[d] + n - it_start[d], 0).astype(i32)
    item_first = jnp.where(live, n == it_start[d], True).astype(i32)
    item_expert = jnp.where(live, e_d[d], 0).astype(i32)

    before = jnp.concatenate([jnp.zeros((1, N_EXPERTS), i32), csum], axis=0)[0:m:MOE_BLK]
    blk_a = ((goff[None, :] + before) // MOE_BLK).astype(i32).reshape(-1)
    return dict(pos=pos, pos_t=pos.T, tile_expert=tile_expert, tile_active=tile_active,
                item_tile=item_tile, item_blk=item_blk, item_first=item_first,
                item_expert=item_expert, blk_a=blk_a, n_pos=n_pos)


def _moe_gather_kernel(it_ref, ib_ref, if_ref, ie_ref, post_ref, h_ref, o_ref):
    n = pl.program_id(0)
    rel = post_ref[pl.ds(ie_ref[n], 1), :] - it_ref[n] * MOE_BLK
    row = lax.broadcasted_iota(jnp.int32, (MOE_BLK, MOE_SRC_BLK), 0)
    onehot = jnp.where(rel == row, 1.0, 0.0).astype(BF16)
    res = _dot(onehot, h_ref[...]).astype(o_ref.dtype)

    @pl.when(if_ref[n] == 1)
    def _():
        o_ref[...] = res

    @pl.when(if_ref[n] == 0)
    def _():
        o_ref[...] += res


def _moe_gather(route, h2):
    n_rows = route["n_pos"] + MOE_BLK
    n_items = route["item_tile"].shape[0]
    grid_spec = pltpu.PrefetchScalarGridSpec(
        num_scalar_prefetch=4,
        grid=(n_items,),
        in_specs=[
            pl.BlockSpec((N_EXPERTS, MOE_SRC_BLK), lambda n, it, ib, fi, ie: (0, ib[n])),
            pl.BlockSpec((MOE_SRC_BLK, D_MODEL), lambda n, it, ib, fi, ie: (ib[n], 0)),
        ],
        out_specs=pl.BlockSpec((MOE_BLK, D_MODEL), lambda n, it, ib, fi, ie: (it[n], 0)),
    )
    return pl.pallas_call(
        _moe_gather_kernel,
        grid_spec=grid_spec,
        out_shape=jax.ShapeDtypeStruct((n_rows, D_MODEL), BF16),
        compiler_params=_cparams(("arbitrary",)),
        name="moe_gather",
    )(route["item_tile"], route["item_blk"], route["item_first"], route["item_expert"],
      route["pos_t"], h2)


def _moe_ffn_kernel(te_ref, ta_ref, x_ref, w1_ref, w3_ref, w2_ref, o_ref, acc_ref):
    k = pl.program_id(0)
    f = pl.program_id(1)
    nf = pl.num_programs(1)
    active = ta_ref[k] == 1

    @pl.when(active)
    def _():
        part = _swiglu_part(x_ref[...], w1_ref[...], w3_ref[...], w2_ref[...].astype(BF16))

        @pl.when(f == 0)
        def _():
            acc_ref[...] = part

        @pl.when(f > 0)
        def _():
            acc_ref[...] += part

        @pl.when(f == nf - 1)
        def _():
            o_ref[...] = acc_ref[...].astype(o_ref.dtype)

    @pl.when(jnp.logical_not(active) & (f == nf - 1))
    def _():
        o_ref[...] = jnp.zeros_like(o_ref)


def _moe_ffn(route, xg, w1, w3, w2):
    n_pos = route["n_pos"]
    d_ff = w1.shape[2]
    grid_spec = pltpu.PrefetchScalarGridSpec(
        num_scalar_prefetch=2,
        grid=(n_pos // MOE_TILE, d_ff // MOE_FF_TILE),
        in_specs=[
            pl.BlockSpec((MOE_TILE, D_MODEL), lambda k, f, te, ta: (k, 0)),
            pl.BlockSpec((None, D_MODEL, MOE_FF_TILE), lambda k, f, te, ta: (te[k], 0, f)),
            pl.BlockSpec((None, D_MODEL, MOE_FF_TILE), lambda k, f, te, ta: (te[k], 0, f)),
            pl.BlockSpec((None, MOE_FF_TILE, D_MODEL), lambda k, f, te, ta: (te[k], f, 0)),
        ],
        out_specs=pl.BlockSpec((MOE_TILE, D_MODEL), lambda k, f, te, ta: (k, 0)),
        scratch_shapes=[pltpu.VMEM((MOE_TILE, D_MODEL), F32)],
    )
    return pl.pallas_call(
        _moe_ffn_kernel,
        grid_spec=grid_spec,
        out_shape=jax.ShapeDtypeStruct((n_pos, D_MODEL), BF16),
        compiler_params=_cparams(("arbitrary", "arbitrary")),
        name="moe_expert_ffn",
    )(route["tile_expert"], route["tile_active"], xg, w1, w3, w2)


def _moe_combine_kernel(ba_ref, x_ref, gate_ref, pos_ref, *refs):
    y_refs, o_ref = refs[:-1], refs[-1]
    tb = pl.program_id(0)
    lane = lax.broadcasted_iota(jnp.int32, (MOE_BLK, MOE_BLK), 1)
    out = x_ref[...]
    for e in range(N_EXPERTS):
        base = ba_ref[tb * N_EXPERTS + e] * MOE_BLK
        rel = pos_ref[:, e:e + 1] - base
        picked = None
        for half in range(2):
            onehot = jnp.where(rel - half * MOE_BLK == lane, 1.0, 0.0).astype(BF16)
            part = _dot(onehot, y_refs[2 * e + half][...])
            picked = part if picked is None else picked + part
        out = out + gate_ref[:, e:e + 1] * picked
    o_ref[...] = out


def _moe_combine(route, x1, gate, y):
    m = x1.shape[0]
    last_blk = route["n_pos"] // MOE_BLK - 1
    row = lambda tb, ba: (tb, 0)
    y_specs = []
    for e in range(N_EXPERTS):
        for half in range(2):
            y_specs.append(pl.BlockSpec(
                (MOE_BLK, D_MODEL),
                lambda tb, ba, e=e, half=half: (
                    jnp.minimum(ba[tb * N_EXPERTS + e] + half, last_blk), 0)))
    grid_spec = pltpu.PrefetchScalarGridSpec(
        num_scalar_prefetch=1,
        grid=(m // MOE_BLK,),
        in_specs=[
            pl.BlockSpec((MOE_BLK, D_MODEL), row),
            pl.BlockSpec((MOE_BLK, LANES), row),
            pl.BlockSpec((MOE_BLK, N_EXPERTS), row),
        ] + y_specs,
        out_specs=pl.BlockSpec((MOE_BLK, D_MODEL), row),
    )
    return pl.pallas_call(
        _moe_combine_kernel,
        grid_spec=grid_spec,
        out_shape=jax.ShapeDtypeStruct((m, D_MODEL), F32),
        compiler_params=_cparams(("arbitrary",)),
        name="moe_combine",
    )(route["blk_a"], x1, gate, route["pos"], *([y] * (2 * N_EXPERTS)))


def _moe(h2, x1, gate, w1, w3, w2):
    route = _moe_route(gate)
    xg = _moe_gather(route, h2)
    y = _moe_ffn(route, xg, w1, w3, w2)
    return _moe_combine(route, x1, gate, y)


WPREP_ROWS = 256
IN_PROJ_PIECES = ((0, 0, 2560), (COL_QI, 2688, 256), (COL_KV, 2560, 128), (COL_KIW, 2944, 128),
                  (COL_GATES, 3012, 3072))


def _wprep_kernel(w_ref, o_ref):
    for dst, src, width in IN_PROJ_PIECES:
        o_ref[:, dst:dst + width] = w_ref[:, src:src + width].astype(o_ref.dtype)


def _pad_in_proj(w_in):
    depth, d, in_cols = w_in.shape
    return pl.pallas_call(
        _wprep_kernel,
        grid=(depth, d // WPREP_ROWS),
        in_specs=[pl.BlockSpec((None, WPREP_ROWS, in_cols), lambda l, i: (l, i, 0))],
        out_specs=pl.BlockSpec((None, WPREP_ROWS, PROJ_COLS), lambda l, i: (l, i, 0)),
        out_shape=jax.ShapeDtypeStruct((depth, d, PROJ_COLS), BF16),
        compiler_params=_cparams(("parallel", "parallel")),
        name="in_proj_weight_layout",
    )(w_in)


def _s5_params(lam_re, lam_im, log_dt, b_re, b_im, c_re, c_im):
    lr = jnp.minimum(lam_re.astype(F32), -1e-4)
    li = lam_im.astype(F32)
    dt = jnp.exp(log_dt.astype(F32))[:, None]
    mag = jnp.exp(lr * dt)
    ang = li * dt
    ab_re = mag * jnp.cos(ang)
    ab_im = mag * jnp.sin(ang)
    nr = ab_re - 1.0
    ni = ab_im
    den = lr * lr + li * li
    coef_re = (nr * lr + ni * li) / den
    coef_im = (ni * lr - nr * li) / den
    bf_re = coef_re[:, :, None] * b_re - coef_im[:, :, None] * b_im
    bf_im = coef_re[:, :, None] * b_im + coef_im[:, :, None] * b_re
    gh = SSM_GROUPS // SSM_HALVES
    eye = jnp.eye(gh, dtype=F32)

    def blockdiag_b(w):
        w = w.reshape(SSM_HALVES, gh, SSM_STATE, SSM_GROUP)
        return jnp.einsum('jgph,gk->jghkp', w, eye).reshape(SSM_HALVES, SSM_HALF_CH, SSM_HALF_STATES)

    def blockdiag_c(w):
        w = w.reshape(SSM_HALVES, gh, SSM_GROUP, SSM_STATE)
        return jnp.einsum('jghp,gk->jgpkh', w, eye).reshape(SSM_HALVES, SSM_HALF_STATES, SSM_HALF_CH)

    bc = jnp.concatenate([blockdiag_b(bf_re), blockdiag_b(bf_im)], axis=2).astype(BF16)
    cc = jnp.concatenate([blockdiag_c(c_re.astype(F32)), -blockdiag_c(c_im.astype(F32))],
                         axis=1).astype(BF16)
    a_re8 = jnp.broadcast_to(ab_re.reshape(1, -1), (SUBLANES, SSM_GROUPS * SSM_STATE))
    a_im8 = jnp.broadcast_to(ab_im.reshape(1, -1), (SUBLANES, SSM_GROUPS * SSM_STATE))
    return bc, a_re8, a_im8, cc


def _rope_tables(seq):
    pos = jnp.arange(seq, dtype=F32)
    inv = ROPE_THETA ** (-jnp.arange(HEAD_DIM // 2, dtype=F32) / (HEAD_DIM // 2))
    ang = pos[:, None] * inv[None, :]
    cos, sin = jnp.cos(ang), jnp.sin(ang)
    cos2 = jnp.concatenate([cos, cos], axis=-1)
    sin2 = jnp.concatenate([-sin, sin], axis=-1)
    one, zero = jnp.ones_like(cos2), jnp.zeros_like(sin2)
    tab_a = jnp.concatenate([cos2, cos2, sin2, sin2], axis=-1)
    tab_b = jnp.concatenate([cos2, one, sin2, zero], axis=-1)
    return tab_a, tab_b


def kernel(x, ln1_g, w_in, conv_w, ssm_lam_re, ssm_lam_im, ssm_log_dt, ssm_b_re, ssm_b_im,
           ssm_c_re, ssm_c_im, ssm_d, ssm_w_glu, q_norm_g, k_norm_g, w_out_a, w_out_b, w_out_c,
           w_o, ln2_g, ffn_w1, ffn_w3, ffn_w2, router_w, moe_w1, moe_w3, moe_w2):
    batch, seq, d = x.shape
    assert batch == SUBLANES and d == D_MODEL
    depth = w_in.shape[0]
    m = batch * seq
    tab_a, tab_b = _rope_tables(seq)
    ii = lax.broadcasted_iota(jnp.int32, (LANES, LANES), 0)
    jj = lax.broadcasted_iota(jnp.int32, (LANES, LANES), 1)
    tri = (ii < jj).astype(BF16)
    gsum = ((ii // HEAD_DIM) == (jj // HEAD_DIM)).astype(BF16)
    r_tm = lax.broadcasted_iota(jnp.int32, (MIX_ROWS, MIX_ROWS), 0)
    r_bm = lax.broadcasted_iota(jnp.int32, (MIX_ROWS, MIX_ROWS), 1)
    perm = ((r_tm % SUBLANES) * MIX_STEPS + r_tm // SUBLANES == r_bm).astype(BF16)
    perm_t = perm.T

    w_in_p = _pad_in_proj(w_in)
    xt = x.reshape(m, d)
    for layer in range(depth):
        q_gain2 = jnp.tile(q_norm_g[layer], 2)[None, :]
        k_gain2 = jnp.tile(k_norm_g[layer], 2)[None, :]
        proj = _inproj(xt, ln1_g[layer][None, :], w_in_p, layer, tab_a, tab_b,
                       q_gain2, k_gain2, gsum, seq)
        proj3 = proj.reshape(batch, seq, PROJ_COLS)
        bc, a_re8, a_im8, cc = _s5_params(
            ssm_lam_re[layer], ssm_lam_im[layer], ssm_log_dt[layer], ssm_b_re[layer],
            ssm_b_im[layer], ssm_c_re[layer], ssm_c_im[layer])
        mab = _mixer(proj3, conv_w[layer], perm, perm_t, bc, a_re8, a_im8, cc,
                     ssm_d[layer][None, :], ssm_w_glu[layer].astype(BF16),
                     w_out_a[layer].astype(BF16), w_out_b[layer].astype(BF16))
        mab = mab.reshape(m, D_MODEL)
        attn = _dsa(proj3, tri).reshape(m, ATTN_WIDTH)
        j = layer // 2
        if layer % 2 == 0:
            x1, h2 = _merge(attn, mab, proj, xt, w_out_c[layer].astype(BF16),
                            w_o[layer].astype(BF16), ln2_g[layer][None, :], None)
            xt = _ffn(h2, x1, ffn_w1[j].astype(BF16), ffn_w3[j].astype(BF16),
                      ffn_w2[j].astype(BF16))
        else:
            rw = jnp.pad(router_w[j], ((0, 0), (0, LANES - N_EXPERTS))).astype(BF16)
            x1, h2, gate = _merge(attn, mab, proj, xt, w_out_c[layer].astype(BF16),
                                  w_o[layer].astype(BF16), ln2_g[layer][None, :], rw)
            xt = _moe(h2, x1, gate, moe_w1[j].astype(BF16), moe_w3[j].astype(BF16), moe_w2[j])
    return xt.reshape(batch, seq, d)
```
